```python
import jax, jax.numpy as jnp
from jax import lax
import numpy as np

D_MODEL = 2048
BATCH = 2
SEQ = 4096
DEPTH = 1
DEC_BATCH = 8
DEC_SEQ = 1
PAST_LEN = 16384
PAGE_SIZE = 128

HEAD_DIM = 128
N_MIX_HEADS = D_MODEL // HEAD_DIM
SB_HEADS = N_MIX_HEADS // 2
GDN_HEADS = N_MIX_HEADS - SB_HEADS
SB_WIDTH = SB_HEADS * HEAD_DIM
GDN_WIDTH = GDN_HEADS * HEAD_DIM
MIX_WIDTH = SB_WIDTH + GDN_WIDTH
SB_BLOCK = 128
SB_SCALE = HEAD_DIM ** -0.5
SB_LOGIT_BIAS_INIT = -8.0
GDN_CONV = 4
GDN_CONV_CH = 3 * GDN_WIDTH
GDN_CHUNK = 64
D_FF = ((8 * D_MODEL // 3 + 127) // 128) * 128
FFN_CONV = 3
PLE_DIM = 256
EPS = 1e-6

O_SB_Q = 0
O_SB_K = O_SB_Q + SB_WIDTH
O_SB_V = O_SB_K + SB_WIDTH
O_GDN_QKV = O_SB_V + SB_WIDTH
O_GDN_Z = O_GDN_QKV + GDN_CONV_CH
O_GDN_A = O_GDN_Z + GDN_WIDTH
O_GDN_B = O_GDN_A + GDN_HEADS
IN_COLS = O_GDN_B + GDN_HEADS

kernel_name = 'hymba_stickbreak_gdn_convffn_step'


def rms_norm(x, w):
    x32 = x.astype(jnp.float32)
    y = x32 * lax.rsqrt(jnp.mean(x32 * x32, axis=-1, keepdims=True) + EPS)
    return (y * w.astype(jnp.float32)).astype(x.dtype)


def l2_normalize(x):
    return x * lax.rsqrt(jnp.sum(x * x, axis=-1, keepdims=True) + 1e-6)


def causal_dwconv(xh, w):
    width = w.shape[0]
    t = xh.shape[1] - (width - 1)
    y = xh[:, 0:t] * w[0]
    for j in range(1, width):
        y = y + xh[:, j:j + t] * w[j]
    return y.astype(xh.dtype)


def sb_block(q_blk, k, v, logit_bias, q_start):
    tq, tk = q_blk.shape[1], k.shape[1]
    z = (jnp.einsum('bqhd,bkhd->bhqk', q_blk, k).astype(jnp.float32) * SB_SCALE
         + logit_bias.astype(jnp.float32)[None, :, None, None])
    qpos = q_start + jnp.arange(tq)
    kpos = jnp.arange(tk)
    valid = kpos[None, :] < qpos[:, None]
    log_beta = jax.nn.log_sigmoid(z)
    log_1m_beta = jnp.where(valid, log_beta - z, 0.0)
    between = lax.cumsum(log_1m_beta, axis=3, reverse=True) - log_1m_beta
    a = jnp.where(valid, jnp.exp(log_beta + between), 0.0)
    return jnp.einsum('bhqk,bkhd->bqhd', a.astype(v.dtype), v)


def sb_attention(q, k, v, logit_bias, q_offset):
    tq, tk = q.shape[1], k.shape[1]
    outs = []
    for start in range(0, tq, SB_BLOCK):
        stop = min(start + SB_BLOCK, tq)
        kend = min(tk, q_offset + stop)
        outs.append(sb_block(q[:, start:stop], k[:, :kend], v[:, :kend], logit_bias,
                             q_offset + start))
    return jnp.concatenate(outs, axis=1)


def to_chunks(x):
    b, tp, h = x.shape[:3]
    x = x.reshape((b, tp // GDN_CHUNK, GDN_CHUNK, h) + x.shape[3:])
    return jnp.moveaxis(x, (1, 3), (0, 2))


def gdn_chunked(q, k, v, g, beta, s0):
    b, t, h, dv = v.shape
    pad = (-t) % GDN_CHUNK
    if pad:
        padt = lambda a: jnp.pad(a, [(0, 0), (0, pad)] + [(0, 0)] * (a.ndim - 2))
        q, k, v, g, beta = padt(q), padt(k), padt(v), padt(g), padt(beta)
    q, k, v, g, beta = to_chunks(q), to_chunks(k), to_chunks(v), to_chunks(g), to_chunks(beta)
    gc = jnp.cumsum(g, axis=-1)
    kb = k * beta[..., None]
    vb = v * beta[..., None]
    idx = jnp.arange(GDN_CHUNK)
    tri = idx[:, None] >= idx[None, :]
    strict = idx[:, None] > idx[None, :]
    diff = gc[..., :, None] - gc[..., None, :]
    decay = jnp.where(tri, jnp.exp(jnp.where(tri, diff, 0.0)), 0.0)
    low = jnp.where(strict, jnp.einsum('nbhcd,nbhsd->nbhcs', kb, k) * decay, 0.0)
    unit_low = low + jnp.eye(GDN_CHUNK, dtype=low.dtype)
    rhs = jnp.concatenate([vb, kb * jnp.exp(gc)[..., None]], axis=-1)
    sol = lax.linalg.triangular_solve(unit_low, rhs, left_side=True, lower=True,
                                      unit_diagonal=True)
    u, w = sol[..., :dv], sol[..., dv:]
    qk = jnp.einsum('nbhcd,nbhsd->nbhcs', q, k) * decay

    def step(s, inp):
        q_i, k_i, u_i, w_i, qk_i, gc_i = inp
        v_new = u_i - jnp.einsum('bhcd,bhde->bhce', w_i, s)
        o_i = (jnp.einsum('bhcd,bhde->bhce', q_i * jnp.exp(gc_i)[..., None], s)
               + jnp.einsum('bhcs,bhse->bhce', qk_i, v_new))
        g_last = gc_i[..., -1:]
        s = (s * jnp.exp(g_last)[..., None]
             + jnp.einsum('bhcd,bhce->bhde', k_i * jnp.exp(g_last - gc_i)[..., None], v_new))
        return s, o_i

    s_final, o = lax.scan(step, s0, (q, k, u, w, qk, gc))
    o = jnp.moveaxis(o, (0, 2), (1, 3)).reshape(b, -1, h, dv)[:, :t]
    return o, s_final


def layer_forward(x, p, k_past, v_past, gdn_conv_hist, gdn_s0, ffn_conv_hist, q_offset,
                  attn_norm, w_in, sb_logit_bias, sb_out_norm, gdn_conv_w, gdn_a_log,
                  gdn_dt_bias, gdn_out_norm, w_out, ffn_norm, w_ffn_gate, w_ffn_up,
                  ffn_conv_w, w_ffn_down, ple_norm, w_ple_gate, w_ple_proj):
    b, t, _ = x.shape
    f32 = jnp.float32
    a = rms_norm(x, attn_norm)
    proj = a @ w_in

    sb_q = proj[..., O_SB_Q:O_SB_Q + SB_WIDTH].reshape(b, t, SB_HEADS, HEAD_DIM)
    sb_k = proj[..., O_SB_K:O_SB_K + SB_WIDTH].reshape(b, t, SB_HEADS, HEAD_DIM)
    sb_v = proj[..., O_SB_V:O_SB_V + SB_WIDTH].reshape(b, t, SB_HEADS, HEAD_DIM)
    if k_past is None:
        k_all, v_all = sb_k, sb_v
    else:
        k_all = jnp.concatenate([k_past.astype(sb_k.dtype), sb_k], axis=1)
        v_all = jnp.concatenate([v_past.astype(sb_v.dtype), sb_v], axis=1)
    o_sb = rms_norm(sb_attention(sb_q, k_all, v_all, sb_logit_bias, q_offset), sb_out_norm)

    conv_in = proj[..., O_GDN_QKV:O_GDN_QKV + GDN_CONV_CH]
    hist = jnp.concatenate([gdn_conv_hist.astype(conv_in.dtype), conv_in], axis=1)
    new_gdn_conv = hist[:, hist.shape[1] - (GDN_CONV - 1):]
    c = jax.nn.silu(causal_dwconv(hist, gdn_conv_w).astype(f32))
    gq = c[..., 0:GDN_WIDTH].reshape(b, t, GDN_HEADS, HEAD_DIM)
    gk = c[..., GDN_WIDTH:2 * GDN_WIDTH].reshape(b, t, GDN_HEADS, HEAD_DIM)
    gv = c[..., 2 * GDN_WIDTH:3 * GDN_WIDTH].reshape(b, t, GDN_HEADS, HEAD_DIM)
    gq = l2_normalize(gq) * (HEAD_DIM ** -0.5)
    gk = l2_normalize(gk)
    a_in = proj[..., O_GDN_A:O_GDN_A + GDN_HEADS].astype(f32)
    g = -jnp.exp(gdn_a_log.astype(f32)) * jax.nn.softplus(a_in + gdn_dt_bias.astype(f32))
    beta = jax.nn.sigmoid(proj[..., O_GDN_B:O_GDN_B + GDN_HEADS].astype(f32))
    o_gdn, s_new = gdn_chunked(gq, gk, gv, g, beta, gdn_s0.astype(f32))
    z = proj[..., O_GDN_Z:O_GDN_Z + GDN_WIDTH].reshape(b, t, GDN_HEADS, HEAD_DIM).astype(f32)
    o_gdn = rms_norm(o_gdn, gdn_out_norm) * jax.nn.silu(z)

    mix = jnp.concatenate([o_sb.reshape(b, t, SB_WIDTH).astype(x.dtype),
                           o_gdn.reshape(b, t, GDN_WIDTH).astype(x.dtype)], axis=-1)
    h = x + mix @ w_out

    f = rms_norm(h, ffn_norm)
    gate_pre = f @ w_ffn_gate
    ghist = jnp.concatenate([ffn_conv_hist.astype(gate_pre.dtype), gate_pre], axis=1)
    new_ffn_conv = ghist[:, ghist.shape[1] - (FFN_CONV - 1):]
    gate = causal_dwconv(ghist, ffn_conv_w)
    h = h + (jax.nn.silu(gate) * (f @ w_ffn_up)) @ w_ffn_down

    ple_gate = jax.nn.sigmoid(rms_norm(h, ple_norm) @ w_ple_gate)
    h = h + (p @ w_ple_proj) * ple_gate
    return h, sb_k, sb_v, new_gdn_conv, s_new.astype(gdn_s0.dtype), new_ffn_conv


def setup_inputs(seed: int = 0) -> dict:
    key = jax.random.key(seed)
    ks = jax.random.split(key, 32)
    f32 = jnp.float32
    n_pages = PAST_LEN // PAGE_SIZE
    n_pool = (DEC_BATCH * n_pages * 5) // 4

    def nrm(k, shape, scale=1.0):
        return jax.random.normal(k, shape, f32) * scale

    def gain(k, shape):
        return 1.0 + 0.01 * jax.random.normal(k, shape, f32)

    page_table = jax.random.permutation(ks[0], n_pool)[:DEC_BATCH * n_pages]
    page_table = page_table.reshape(DEC_BATCH, n_pages).astype(jnp.int32)
    dt = jnp.exp(jax.random.uniform(ks[1], (DEPTH, GDN_HEADS), f32,
                                    float(np.log(1e-3)), float(np.log(1e-1))))
    gdn_dt_bias = dt + jnp.log(-jnp.expm1(-dt))
    gdn_a_log = jnp.log(jax.random.uniform(ks[2], (DEPTH, GDN_HEADS), f32, 1.0, 16.0))
    return {
        'x_prompt': nrm(ks[3], (BATCH, SEQ, D_MODEL)),
        'x_sample': nrm(ks[4], (DEC_BATCH, DEC_SEQ, D_MODEL)),
        'cache_sb_k': nrm(ks[5], (DEPTH, n_pool, PAGE_SIZE, SB_HEADS, HEAD_DIM)),
        'cache_sb_v': nrm(ks[6], (DEPTH, n_pool, PAGE_SIZE, SB_HEADS, HEAD_DIM)),
        'page_table': page_table,
        'state_gdn_conv': nrm(ks[7], (DEPTH, DEC_BATCH, GDN_CONV - 1, GDN_CONV_CH)),
        'state_gdn_rec': nrm(ks[8], (DEPTH, DEC_BATCH, GDN_HEADS, HEAD_DIM, HEAD_DIM), 0.1),
        'state_ffn_conv': nrm(ks[9], (DEPTH, DEC_BATCH, FFN_CONV - 1, D_FF)),
        'p_prompt': nrm(ks[10], (DEPTH, BATCH, SEQ, PLE_DIM)),
        'p_sample': nrm(ks[11], (DEPTH, DEC_BATCH, DEC_SEQ, PLE_DIM)),
        'attn_norm': gain(ks[12], (DEPTH, D_MODEL)),
        'w_in': nrm(ks[13], (DEPTH, D_MODEL, IN_COLS), D_MODEL ** -0.5),
        'sb_logit_bias': SB_LOGIT_BIAS_INIT + 0.1 * jax.random.normal(ks[27], (DEPTH, SB_HEADS), f32),
        'sb_out_norm': gain(ks[14], (DEPTH, HEAD_DIM)),
        'gdn_conv_w': nrm(ks[15], (DEPTH, GDN_CONV, GDN_CONV_CH), GDN_CONV ** -0.5),
        'gdn_a_log': gdn_a_log,
        'gdn_dt_bias': gdn_dt_bias,
        'gdn_out_norm': gain(ks[16], (DEPTH, HEAD_DIM)),
        'w_out': nrm(ks[17], (DEPTH, MIX_WIDTH, D_MODEL), MIX_WIDTH ** -0.5),
        'ffn_norm': gain(ks[18], (DEPTH, D_MODEL)),
        'w_ffn_gate': nrm(ks[19], (DEPTH, D_MODEL, D_FF), D_MODEL ** -0.5),
        'w_ffn_up': nrm(ks[20], (DEPTH, D_MODEL, D_FF), D_MODEL ** -0.5),
        'ffn_conv_w': nrm(ks[21], (DEPTH, FFN_CONV, D_FF), FFN_CONV ** -0.5),
        'w_ffn_down': nrm(ks[22], (DEPTH, D_FF, D_MODEL), D_FF ** -0.5),
        'ple_norm': gain(ks[23], (DEPTH, D_MODEL)),
        'w_ple_gate': nrm(ks[24], (DEPTH, D_MODEL, D_MODEL), D_MODEL ** -0.5),
        'w_ple_proj': nrm(ks[25], (DEPTH, PLE_DIM, D_MODEL), PLE_DIM ** -0.5),
        'final_norm': gain(ks[26], (D_MODEL,)),
    }


def reference(x_prompt, x_sample, cache_sb_k, cache_sb_v, page_table, state_gdn_conv,
              state_gdn_rec, state_ffn_conv, p_prompt, p_sample, attn_norm, w_in,
              sb_logit_bias, sb_out_norm, gdn_conv_w, gdn_a_log, gdn_dt_bias, gdn_out_norm,
              w_out, ffn_norm, w_ffn_gate, w_ffn_up, ffn_conv_w, w_ffn_down, ple_norm,
              w_ple_gate, w_ple_proj, final_norm):
    bp = x_prompt.shape[0]
    bs = x_sample.shape[0]
    past_len = page_table.shape[1] * cache_sb_k.shape[2]
    hp, hs = x_prompt, x_sample
    out_p = ([], [], [], [], [])
    out_s = ([], [], [], [], [])
    for i in range(DEPTH):
        lw = (attn_norm[i], w_in[i], sb_logit_bias[i], sb_out_norm[i], gdn_conv_w[i],
              gdn_a_log[i], gdn_dt_bias[i], gdn_out_norm[i], w_out[i], ffn_norm[i],
              w_ffn_gate[i], w_ffn_up[i], ffn_conv_w[i], w_ffn_down[i], ple_norm[i],
              w_ple_gate[i], w_ple_proj[i])
        hp, *st_p = layer_forward(
            hp, p_prompt[i], None, None,
            jnp.zeros((bp, GDN_CONV - 1, GDN_CONV_CH), hp.dtype),
            jnp.zeros((bp, GDN_HEADS, HEAD_DIM, HEAD_DIM), hp.dtype),
            jnp.zeros((bp, FFN_CONV - 1, D_FF), hp.dtype), 0, *lw)
        for lst, s in zip(out_p, st_p):
            lst.append(s)
        k_past = cache_sb_k[i][page_table].reshape(bs, past_len, SB_HEADS, HEAD_DIM)
        v_past = cache_sb_v[i][page_table].reshape(bs, past_len, SB_HEADS, HEAD_DIM)
        hs, *st_s = layer_forward(
            hs, p_sample[i], k_past, v_past, state_gdn_conv[i], state_gdn_rec[i],
            state_ffn_conv[i], past_len, *lw)
        for lst, s in zip(out_s, st_s):
            lst.append(s)
    y_prompt = rms_norm(hp, final_norm)
    y_sample = rms_norm(hs, final_norm)
    sb_k_p, sb_v_p, gdn_conv_p, gdn_rec_p, ffn_conv_p = [jnp.stack(l, axis=0) for l in out_p]
    sb_k_s, sb_v_s, gdn_conv_s, gdn_rec_s, ffn_conv_s = [jnp.stack(l, axis=0) for l in out_s]
    return (y_prompt, y_sample, sb_k_p, sb_v_p, gdn_conv_p, gdn_rec_p, ffn_conv_p,
            sb_k_s, sb_v_s, gdn_conv_s, gdn_rec_s, ffn_conv_s)
```

```python
import functools

import jax
import jax.numpy as jnp
from jax import lax
from jax.experimental import pallas as pl
from jax.experimental.pallas import tpu as pltpu

F32 = jnp.float32
BF16 = jnp.bfloat16

EPS = 1e-6
HEAD_DIM = 128
SB_HEADS = 8
GDN_HEADS = 8
SB_WIDTH = SB_HEADS * HEAD_DIM
GDN_WIDTH = GDN_HEADS * HEAD_DIM
GDN_CONV = 4
GDN_CONV_CH = 3 * GDN_WIDTH
FFN_CONV = 3
SB_SCALE = HEAD_DIM ** -0.5
O_GDN_QKV = 3 * SB_WIDTH
O_GDN_Z = O_GDN_QKV + GDN_CONV_CH
PROJ_MAIN = O_GDN_Z + GDN_WIDTH

LANES = 128
SUBLANES = 8
VMEM_LIMIT = 56 * 1024 * 1024
COL_TILE = 512
ROW_TILE = 512
SB_TQ = 512
GDN_C = 128
A_LANE = LANES - 2 * GDN_HEADS
B_LANE = LANES - GDN_HEADS

_NT = (((1,), (1,)), ((), ()))


def _params(*sem):
    return pltpu.CompilerParams(dimension_semantics=sem, vmem_limit_bytes=VMEM_LIMIT)


def _dot(a, b):
    return jnp.dot(a, b, preferred_element_type=F32)


def _dot_nt(a, b):
    return lax.dot_general(a, b, _NT, preferred_element_type=F32)


def _softplus(x):
    return jnp.maximum(x, 0.0) + jnp.log1p(jnp.exp(-jnp.abs(x)))


def _sigmoid(x):
    return 1.0 / (1.0 + jnp.exp(-x))


def _split2(x):
    hi = x.astype(BF16)
    lo = (x - hi.astype(F32)).astype(BF16)
    return hi, lo


def _split3(x):
    h1 = x.astype(BF16)
    r1 = x - h1.astype(F32)
    h2 = r1.astype(BF16)
    h3 = (r1 - h2.astype(F32)).astype(BF16)
    return h1, h2, h3


def _rmsnorm_kernel(x_ref, w_ref, o_ref):
    x = x_ref[...]
    y = x * lax.rsqrt(jnp.mean(x * x, axis=-1, keepdims=True) + EPS)
    o_ref[...] = (y * w_ref[...]).astype(o_ref.dtype)


def _rmsnorm(x, w, out_dtype, tm):
    m, d = x.shape
    return pl.pallas_call(
        _rmsnorm_kernel,
        grid=(m // tm,),
        in_specs=[pl.BlockSpec((tm, d), lambda i: (i, 0)),
                  pl.BlockSpec((1, d), lambda i: (0, 0))],
        out_specs=pl.BlockSpec((tm, d), lambda i: (i, 0)),
        out_shape=jax.ShapeDtypeStruct((m, d), out_dtype),
        compiler_params=_params("parallel"),
        name="rmsnorm",
    )(x, w.reshape(1, d))


def _proj_kernel(a_ref, w_ref, o_ref, wbf_ref):
    @pl.when(pl.program_id(1) == 0)
    def _():
        wbf_ref[...] = w_ref[...].astype(BF16)

    o_ref[...] = _dot(a_ref[...], wbf_ref[...])


def _proj(a, w_in, tm):
    m, d = a.shape
    tn = COL_TILE
    return pl.pallas_call(
        _proj_kernel,
        grid=(PROJ_MAIN // tn, m // tm),
        in_specs=[pl.BlockSpec((tm, d), lambda j, i: (i, 0)),
                  pl.BlockSpec((d, tn), lambda j, i: (0, j))],
        out_specs=pl.BlockSpec((tm, tn), lambda j, i: (i, j)),
        out_shape=jax.ShapeDtypeStruct((m, PROJ_MAIN), F32),
        scratch_shapes=[pltpu.VMEM((d, tn), BF16)],
        compiler_params=_params("arbitrary", "arbitrary"),
        name="proj",
    )(a, w_in)


def _gate_proj_kernel(a_ref, w_ref, wt_ref, ab_ref, abt_ref):
    a = a_ref[...]
    ab_ref[...] = _dot(a, w_ref[...])
    abt_ref[...] = _dot_nt(wt_ref[...], a)


def _gate_proj(a, w_tail, tm):
    m, d = a.shape
    w = w_tail.astype(BF16)
    return pl.pallas_call(
        _gate_proj_kernel,
        grid=(m // tm,),
        in_specs=[pl.BlockSpec((tm, d), lambda i: (i, 0)),
                  pl.BlockSpec((d, LANES), lambda i: (0, 0)),
                  pl.BlockSpec((LANES, d), lambda i: (0, 0))],
        out_specs=[pl.BlockSpec((tm, LANES), lambda i: (i, 0)),
                   pl.BlockSpec((LANES, tm), lambda i: (0, i))],
        out_shape=[jax.ShapeDtypeStruct((m, LANES), F32),
                   jax.ShapeDtypeStruct((LANES, m), F32)],
        compiler_params=_params("parallel"),
        name="gate_proj",
    )(a, w, w.T)


def _cumsum_weights():
    j = lax.broadcasted_iota(jnp.int32, (LANES, LANES), 0)
    s = lax.broadcasted_iota(jnp.int32, (LANES, LANES), 1)
    incl = (j >= s).astype(BF16)
    half = jnp.concatenate([incl, jnp.ones((LANES, LANES), BF16)], axis=1)
    return jnp.concatenate([half, half], axis=0)


def _sb_strip(q, kb, vb, z_bias, w2, r, masked):
    z = _dot_nt(q, kb) + z_bias
    sp = _softplus(z)
    if masked:
        row = lax.broadcasted_iota(jnp.int32, z.shape, 0)
        col = lax.broadcasted_iota(jnp.int32, z.shape, 1)
        valid = col < row
        sp = jnp.where(valid, sp, 0.0)
    hi, lo = _split2(sp)
    res = _dot(jnp.concatenate([hi, lo], axis=1), w2)
    a = jnp.exp(z - res[:, :LANES] - r)
    if masked:
        a = jnp.where(valid, a, 0.0)
    return _dot(a.astype(BF16), vb), r + res[:, LANES:]


def _sb_prompt_kernel(c_ref, q_ref, k_ref, v_ref, w2_ref, nw_ref, o_ref, kbf, vbf, acc, run):
    h = pl.program_id(1)
    i = pl.program_id(2)

    @pl.when(i == 0)
    def _():
        kbf[...] = k_ref[...].astype(BF16)
        vbf[...] = v_ref[...].astype(BF16)

    z_bias = c_ref[h]
    w2 = w2_ref[...]
    q = (q_ref[...] * SB_SCALE).astype(BF16)
    acc[...] = jnp.zeros_like(acc)
    run[...] = jnp.zeros_like(run)
    nsub = SB_TQ // LANES

    def strip(row0, kstart, masked):
        ks = pl.ds(pl.multiple_of(kstart, LANES), LANES)
        rows = slice(row0, SB_TQ)
        d, r = _sb_strip(q[rows], kbf[ks, :], vbf[ks, :], z_bias, w2, run[rows, :], masked)
        acc[rows, :] += d
        run[rows, :] = r

    for c in reversed(range(nsub)):
        strip(c * LANES, i * SB_TQ + c * LANES, True)

    def body(j, carry):
        start = (i - 1 - j) * SB_TQ
        for c in reversed(range(nsub)):
            strip(0, start + c * LANES, False)
        return carry

    lax.fori_loop(0, i, body, 0)

    o = acc[...]
    y = o * lax.rsqrt(jnp.mean(o * o, axis=-1, keepdims=True) + EPS) * nw_ref[...]
    o_ref[...] = y.astype(o_ref.dtype)


def _sb_prompt(proj, logit_bias, out_norm, b, t):
    nq = t // SB_TQ
    blk = lambda rows, fn: pl.BlockSpec((rows, HEAD_DIM), fn)
    return pl.pallas_call(
        _sb_prompt_kernel,
        grid=(b, SB_HEADS, nq),
        in_specs=[pl.BlockSpec(memory_space=pltpu.SMEM),
                  blk(SB_TQ, lambda bi, h, i: (bi * nq + i, h)),
                  blk(t, lambda bi, h, i: (bi, SB_HEADS + h)),
                  blk(t, lambda bi, h, i: (bi, 2 * SB_HEADS + h)),
                  pl.BlockSpec((2 * LANES, 2 * LANES), lambda bi, h, i: (0, 0)),
                  pl.BlockSpec((1, HEAD_DIM), lambda bi, h, i: (0, 0))],
        out_specs=blk(SB_TQ, lambda bi, h, i: (bi * nq + i, h)),
        out_shape=jax.ShapeDtypeStruct((b * t, SB_WIDTH), BF16),
        scratch_shapes=[pltpu.VMEM((t, HEAD_DIM), BF16), pltpu.VMEM((t, HEAD_DIM), BF16),
                        pltpu.VMEM((SB_TQ, HEAD_DIM), F32), pltpu.VMEM((SB_TQ, LANES), F32)],
        compiler_params=_params("arbitrary", "arbitrary", "arbitrary"),
        name="sb_prompt",
    )(logit_bias, proj, proj, proj, _cumsum_weights(), out_norm.reshape(1, HEAD_DIM))


def _sb_decode_kernel(pt_ref, q_ref, kn_ref, vn_ref, k_ref, v_ref, c_ref, w2_ref, nw_ref, o_ref,
                      qbd, acc, run, *, past_len):
    p = pl.program_id(1)
    head_of_lane = lax.broadcasted_iota(jnp.int32, (SB_HEADS, SB_WIDTH), 1) // HEAD_DIM
    own = head_of_lane == lax.broadcasted_iota(jnp.int32, (SB_HEADS, SB_WIDTH), 0)
    z_bias = c_ref[...]

    @pl.when(p == 0)
    def _():
        q = jnp.broadcast_to(q_ref[0] * SB_SCALE, (SB_HEADS, SB_WIDTH))
        qb = jnp.where(own, q, 0.0).astype(BF16)
        qbd[...] = qb
        kn = kn_ref[0].astype(BF16).astype(F32)
        z = jnp.sum(qb.astype(F32) * kn, axis=-1, keepdims=True) + z_bias[:, :1]
        valid = jnp.full(z.shape, past_len, jnp.int32) < past_len
        sp = jnp.where(valid, _softplus(z), 0.0)
        a = jnp.where(valid, jnp.exp(z - sp), 0.0)
        vn = jnp.where(own, vn_ref[0].astype(BF16).astype(F32), 0.0)
        acc[...] = a.astype(BF16).astype(F32) * vn
        run[...] = jnp.broadcast_to(sp, run.shape)

    kp = k_ref[0].astype(BF16)
    vp = v_ref[0].astype(BF16)
    z = _dot_nt(qbd[...], kp) + z_bias
    sp = _softplus(z)
    hi, lo = _split2(sp)
    res = _dot(jnp.concatenate([hi, lo], axis=1), w2_ref[...])
    r = run[...]
    a = jnp.exp(z - res[:, :LANES] - r)
    run[...] = r + res[:, LANES:]
    acc[...] += _dot(a.astype(BF16), vp)

    @pl.when(p == pl.num_programs(1) - 1)
    def _():
        o = jnp.sum(jnp.where(own, acc[...], 0.0), axis=0, keepdims=True)
        nw = nw_ref[...]
        for h in range(SB_HEADS):
            seg = o[:, h * HEAD_DIM:(h + 1) * HEAD_DIM]
            y = seg * lax.rsqrt(jnp.mean(seg * seg, axis=-1, keepdims=True) + EPS) * nw
            o_ref[0, :, h * HEAD_DIM:(h + 1) * HEAD_DIM] = y.astype(o_ref.dtype)


def _sb_decode(proj, cache_k, cache_v, page_table, logit_bias, out_norm):
    b = proj.shape[0]
    n_pool, page, _, _ = cache_k.shape
    n_pages = page_table.shape[1]
    proj3 = proj.reshape(b, 1, PROJ_MAIN)
    ck = cache_k.reshape(n_pool, page, SB_WIDTH)
    cv = cache_v.reshape(n_pool, page, SB_WIDTH)
    row = lambda col: pl.BlockSpec((1, 1, SB_WIDTH), lambda bi, p, pt: (bi, 0, col))
    pg = pl.BlockSpec((1, page, SB_WIDTH),
                      lambda bi, p, pt: (pt[bi * n_pages + n_pages - 1 - p], 0, 0))
    const = lambda shape: pl.BlockSpec(shape, lambda bi, p, pt: (0,) * len(shape))
    grid_spec = pltpu.PrefetchScalarGridSpec(
        num_scalar_prefetch=1,
        grid=(b, n_pages),
        in_specs=[row(0), row(1), row(2), pg, pg,
                  const((SB_HEADS, LANES)), const((2 * LANES, 2 * LANES)), const((1, HEAD_DIM))],
        out_specs=pl.BlockSpec((1, 1, SB_WIDTH), lambda bi, p, pt: (bi, 0, 0)),
        scratch_shapes=[pltpu.VMEM((SB_HEADS, SB_WIDTH), BF16),
                        pltpu.VMEM((SB_HEADS, SB_WIDTH), F32),
                        pltpu.VMEM((SB_HEADS, LANES), F32)],
    )
    out = pl.pallas_call(
        functools.partial(_sb_decode_kernel, past_len=n_pages * page),
        grid_spec=grid_spec,
        out_shape=jax.ShapeDtypeStruct((b, 1, SB_WIDTH), BF16),
        compiler_params=_params("arbitrary", "arbitrary"),
        name="sb_decode",
    )(page_table.reshape(-1), proj3, proj3, proj3, ck, cv,
      jnp.broadcast_to(logit_bias[:, None], (SB_HEADS, LANES)), _cumsum_weights(),
      out_norm.reshape(1, HEAD_DIM))
    return out.reshape(b, SB_WIDTH)


def _shift_rows(x, prev, k):
    xr = pltpu.roll(x, k, axis=0)
    row = lax.broadcasted_iota(jnp.int32, prev.shape, 0)
    top = jnp.where(row < k, pltpu.roll(prev, k, axis=0), xr[:SUBLANES])
    return jnp.concatenate([top, xr[SUBLANES:]], axis=0)


def _gdn_kernel(cin_ref, z_ref, ab_ref, abt_ref, cst_ref, s0_ref, cw_ref, prow_ref, pcol_ref,
                nw_ref, o_ref, sfin_ref, s_scr, hist_scr, *, t_valid, t_total):
    c_len = GDN_C
    t = pl.program_id(1)

    @pl.when(t == 0)
    def _():
        s_scr[...] = s0_ref[0]
        hist_scr[...] = cst_ref[0]

    x = cin_ref[...]
    hist = hist_scr[...]
    cw = cw_ref[...]
    y = x * cw[3:4]
    for kk in range(1, GDN_CONV):
        y = y + _shift_rows(x, hist, kk) * cw[GDN_CONV - 1 - kk:GDN_CONV - kk]
    hist_scr[...] = x[c_len - SUBLANES:]
    conv = y * _sigmoid(y)

    ri = lax.broadcasted_iota(jnp.int32, (c_len, c_len), 0)
    ci = lax.broadcasted_iota(jnp.int32, (c_len, c_len), 1)
    tri = ri >= ci
    strict = ri > ci
    eye = (ri == ci).astype(F32)
    eye_bf = eye.astype(BF16)
    ltri = tri.astype(BF16)
    utri = (ri <= ci).astype(BF16)

    ab = ab_ref[...]
    g_col = -jnp.exp(prow_ref[0:1]) * _softplus(ab + prow_ref[1:2])
    beta_col = _sigmoid(ab)
    abt = abt_ref[0]
    g_row = (-jnp.exp(pcol_ref[0:GDN_HEADS]) *
             _softplus(abt[0:GDN_HEADS] + pcol_ref[GDN_HEADS:2 * GDN_HEADS]))
    if t_valid < t_total:
        pos_c = t * c_len + lax.broadcasted_iota(jnp.int32, (c_len, LANES), 0)
        g_col = jnp.where(pos_c < t_valid, g_col, 0.0)
        beta_col = jnp.where(pos_c < t_valid, beta_col, 0.0)
        pos_r = t * c_len + lax.broadcasted_iota(jnp.int32, (GDN_HEADS, c_len), 1)
        g_row = jnp.where(pos_r < t_valid, g_row, 0.0)
    gc_col = sum(_dot(ltri, part) for part in _split3(g_col))
    gc_row = sum(_dot(part, utri) for part in _split3(g_row))

    levels = []
    m = 1
    while m < c_len:
        lm = m.bit_length() - 1
        same = ((ri ^ ci) >> (lm + 1)) == 0
        lower = ((ri >> lm) & 1) > ((ci >> lm) & 1)
        levels.append(jnp.where(same, jnp.where(lower, 1.0, 0.0), 0.0))
        m *= 2

    nw = nw_ref[...]
    for h in range(GDN_HEADS):
        sl = slice(h * HEAD_DIM, (h + 1) * HEAD_DIM)
        q = conv[:, sl]
        k = conv[:, GDN_WIDTH + h * HEAD_DIM:GDN_WIDTH + (h + 1) * HEAD_DIM]
        v = conv[:, 2 * GDN_WIDTH + h * HEAD_DIM:2 * GDN_WIDTH + (h + 1) * HEAD_DIM]
        qn = q * lax.rsqrt(jnp.sum(q * q, axis=-1, keepdims=True) + 1e-6) * (HEAD_DIM ** -0.5)
        kn = k * lax.rsqrt(jnp.sum(k * k, axis=-1, keepdims=True) + 1e-6)
        gcc = gc_col[:, A_LANE + h:A_LANE + h + 1]
        beta = beta_col[:, B_LANE + h:B_LANE + h + 1]
        gcr = gc_row[h:h + 1, :]
        kb = kn * beta
        vb = v * beta
        decay = jnp.where(tri, jnp.exp(jnp.where(tri, gcc - gcr, 0.0)), 0.0)
        kn_bf = kn.astype(BF16)
        low = jnp.where(strict, _dot_nt(kb.astype(BF16), kn_bf) * decay, 0.0)

        xinv = eye - low * levels[0]
        for lvl in levels[1:]:
            xb = xinv.astype(BF16)
            y1 = _dot(xb, (low * lvl).astype(BF16))
            xinv = xinv - _dot(y1.astype(BF16), xb)
        nlow = (xinv - eye).astype(BF16)

        egc = jnp.exp(gcc)
        rhs = jnp.concatenate([vb, kb * egc], axis=1)
        sol = rhs + _dot(nlow, rhs.astype(BF16))
        u = sol[:, :HEAD_DIM]
        w = sol[:, HEAD_DIM:]
        qk = _dot_nt(qn.astype(BF16), kn_bf) * decay

        s = s_scr[h]
        ws_qs = _dot(jnp.concatenate([w, qn * egc], axis=0).astype(BF16), s.astype(BF16))
        v_new = (u - ws_qs[:c_len]).astype(BF16)
        o = ws_qs[c_len:] + _dot(qk.astype(BF16), v_new)
        g_last = gcc[c_len - 1:c_len, :]
        kd = (kn * jnp.exp(g_last - gcc)).astype(BF16)
        kd_t = _dot_nt(eye_bf, kd).astype(BF16)
        s_scr[h] = s * jnp.exp(g_last) + _dot(kd_t, v_new)

        zg = z_ref[:, sl]
        on = o * lax.rsqrt(jnp.mean(o * o, axis=-1, keepdims=True) + EPS) * nw
        o_ref[:, sl] = (on * (zg * _sigmoid(zg))).astype(o_ref.dtype)

    @pl.when(t == pl.num_programs(1) - 1)
    def _():
        sfin_ref[0] = s_scr[...]


def _gdn(src, cin_blk, z_blk, ab, abt, conv_state, rec_state, conv_w, a_log, dt_bias, out_norm,
         b, t_pad, t_valid):
    c_len = GDN_C
    nt = t_pad // c_len
    cst = jnp.pad(conv_state, ((0, 0), (SUBLANES - (GDN_CONV - 1), 0), (0, 0)))
    cw = jnp.pad(conv_w, ((0, SUBLANES - GDN_CONV), (0, 0)))
    prow = jnp.zeros((SUBLANES, LANES), F32)
    prow = prow.at[0, A_LANE:B_LANE].set(a_log).at[1, A_LANE:B_LANE].set(dt_bias)
    pcol = jnp.broadcast_to(jnp.concatenate([a_log, dt_bias])[:, None], (2 * GDN_HEADS, LANES))
    abt3 = abt[A_LANE:].reshape(2 * GDN_HEADS, b * nt, c_len).transpose(1, 0, 2)
    full = lambda shape: pl.BlockSpec(shape, lambda bi, ti: (0,) * len(shape))
    return pl.pallas_call(
        functools.partial(_gdn_kernel, t_valid=t_valid, t_total=t_pad),
        grid=(b, nt),
        in_specs=[pl.BlockSpec((c_len, GDN_CONV_CH), lambda bi, ti: (bi * nt + ti, cin_blk)),
                  pl.BlockSpec((c_len, GDN_WIDTH), lambda bi, ti: (bi * nt + ti, z_blk)),
                  pl.BlockSpec((c_len, LANES), lambda bi, ti: (bi * nt + ti, 0)),
                  pl.BlockSpec((1, 2 * GDN_HEADS, c_len), lambda bi, ti: (bi * nt + ti, 0, 0)),
                  pl.BlockSpec((1, SUBLANES, GDN_CONV_CH), lambda bi, ti: (bi, 0, 0)),
                  pl.BlockSpec((1, GDN_HEADS, HEAD_DIM, HEAD_DIM), lambda bi, ti: (bi, 0, 0, 0)),
                  full((SUBLANES, GDN_CONV_CH)), full((SUBLANES, LANES)),
                  full((2 * GDN_HEADS, LANES)), full((1, HEAD_DIM))],
        out_specs=[pl.BlockSpec((c_len, GDN_WIDTH), lambda bi, ti: (bi * nt + ti, 0)),
                   pl.BlockSpec((1, GDN_HEADS, HEAD_DIM, HEAD_DIM), lambda bi, ti: (bi, 0, 0, 0))],
        out_shape=[jax.ShapeDtypeStruct((b * t_pad, GDN_WIDTH), BF16),
                   jax.ShapeDtypeStruct((b, GDN_HEADS, HEAD_DIM, HEAD_DIM), F32)],
        scratch_shapes=[pltpu.VMEM((GDN_HEADS, HEAD_DIM, HEAD_DIM), F32),
                        pltpu.VMEM((SUBLANES, GDN_CONV_CH), F32)],
        compiler_params=_params("arbitrary", "arbitrary"),
        name="gdn",
    )(src, src, ab, abt3, cst, rec_state, cw, prow, pcol, out_norm.reshape(1, HEAD_DIM))


def _outproj_kernel(ms_ref, mg_ref, w0_ref, w1_ref, x_ref, o_ref, w0_bf, w1_bf):
    @pl.when(pl.program_id(1) == 0)
    def _():
        w0_bf[...] = w0_ref[...].astype(BF16)
        w1_bf[...] = w1_ref[...].astype(BF16)

    o_ref[...] = x_ref[...] + _dot(ms_ref[...], w0_bf[...]) + _dot(mg_ref[...], w1_bf[...])


def _outproj(mix_sb, mix_gdn, w_out, x, tm):
    m, d = x.shape
    tn = COL_TILE
    return pl.pallas_call(
        _outproj_kernel,
        grid=(d // tn, m // tm),
        in_specs=[pl.BlockSpec((tm, SB_WIDTH), lambda j, i: (i, 0)),
                  pl.BlockSpec((tm, GDN_WIDTH), lambda j, i: (i, 0)),
                  pl.BlockSpec((SB_WIDTH, tn), lambda j, i: (0, j)),
                  pl.BlockSpec((GDN_WIDTH, tn), lambda j, i: (1, j)),
                  pl.BlockSpec((tm, tn), lambda j, i: (i, j))],
        out_specs=pl.BlockSpec((tm, tn), lambda j, i: (i, j)),
        out_shape=jax.ShapeDtypeStruct((m, d), F32),
        scratch_shapes=[pltpu.VMEM((SB_WIDTH, tn), BF16), pltpu.VMEM((GDN_WIDTH, tn), BF16)],
        compiler_params=_params("arbitrary", "arbitrary"),
        name="outproj",
    )(mix_sb, mix_gdn, w_out, w_out, x)


def _ffn_up_kernel(f_ref, wg_ref, wu_ref, cw_ref, hist_ref, act_ref, st_ref, wg_bf, wu_bf, carry,
                   *, tiles_per_seq, single_token):
    i = pl.program_id(1)

    @pl.when(i == 0)
    def _():
        wg_bf[...] = wg_ref[...].astype(BF16)
        wu_bf[...] = wu_ref[...].astype(BF16)

    f = f_ref[...]
    g = _dot(f, wg_bf[...])
    u = _dot(f, wu_bf[...])
    cw = cw_ref[...]
    if single_token:
        h0 = hist_ref[0]
        h1 = hist_ref[1]
        gate = h0 * cw[0:1] + h1 * cw[1:2] + g * cw[2:3]
        st_ref[0] = h1
        st_ref[1] = g
    else:
        @pl.when(i % tiles_per_seq == 0)
        def _():
            carry[...] = hist_ref[0]

        prev = carry[...]
        gate = (_shift_rows(g, prev, 2) * cw[0:1] + _shift_rows(g, prev, 1) * cw[1:2]
                + g * cw[2:3])
        last = g[g.shape[0] - SUBLANES:]
        carry[...] = last
        st_ref[0] = last
    act_ref[...] = (gate * _sigmoid(gate) * u).astype(act_ref.dtype)


def _ffn_up(f, w_gate, w_up, conv_w, state, tm, seq_len):
    m, d = f.shape
    dff = w_gate.shape[1]
    tn = COL_TILE
    nj = pl.cdiv(dff, tn)
    cw = jnp.pad(conv_w, ((0, SUBLANES - FFN_CONV), (0, 0)))
    single = seq_len == 1
    nb = m // seq_len
    if single:
        assert tm == m
        hist = state.transpose(1, 0, 2)
        hist_spec = pl.BlockSpec((FFN_CONV - 1, m, tn), lambda j, i: (0, 0, j))
        st_spec = pl.BlockSpec((FFN_CONV - 1, m, tn), lambda j, i: (0, 0, j))
        st_shape = jax.ShapeDtypeStruct((FFN_CONV - 1, m, dff), F32)
        tiles_per_seq = 1
    else:
        tiles_per_seq = seq_len // tm
        hist = jnp.pad(state, ((0, 0), (SUBLANES - (FFN_CONV - 1), 0), (0, 0)))
        hist_spec = pl.BlockSpec((1, SUBLANES, tn), lambda j, i: (i // tiles_per_seq, 0, j))
        st_spec = pl.BlockSpec((1, SUBLANES, tn), lambda j, i: (i // tiles_per_seq, 0, j))
        st_shape = jax.ShapeDtypeStruct((nb, SUBLANES, dff), F32)
    act, st = pl.pallas_call(
        functools.partial(_ffn_up_kernel, tiles_per_seq=tiles_per_seq, single_token=single),
        grid=(nj, m // tm),
        in_specs=[pl.BlockSpec((tm, d), lambda j, i: (i, 0)),
                  pl.BlockSpec((d, tn), lambda j, i: (0, j)),
                  pl.BlockSpec((d, tn), lambda j, i: (0, j)),
                  pl.BlockSpec((SUBLANES, tn), lambda j, i: (0, j)),
                  hist_spec],
        out_specs=[pl.BlockSpec((tm, tn), lambda j, i: (i, j)), st_spec],
        out_shape=[jax.ShapeDtypeStruct((m, dff), BF16), st_shape],
        scratch_shapes=[pltpu.VMEM((d, tn), BF16), pltpu.VMEM((d, tn), BF16),
                        pltpu.VMEM((SUBLANES, tn), F32)],
        compiler_params=_params("arbitrary", "arbitrary"),
        name="ffn_up",
    )(f, w_gate, w_up, cw, hist)
    if single:
        return act, st.transpose(1, 0, 2)
    return act, st[:, SUBLANES - (FFN_CONV - 1):]


def _ffn_down_kernel(act_ref, w_ref, h_ref, o_ref, w_bf):
    @pl.when(pl.program_id(1) == 0)
    def _():
        w_bf[...] = w_ref[...].astype(BF16)

    o_ref[...] = h_ref[...] + _dot(act_ref[...], w_bf[...])


def _ffn_down(act, w_down, h, tm):
    m, dff = act.shape
    d = h.shape[1]
    tn = COL_TILE
    return pl.pallas_call(
        _ffn_down_kernel,
        grid=(d // tn, m // tm),
        in_specs=[pl.BlockSpec((tm, dff), lambda j, i: (i, 0)),
                  pl.BlockSpec((dff, tn), lambda j, i: (0, j)),
                  pl.BlockSpec((tm, tn), lambda j, i: (i, j))],
        out_specs=pl.BlockSpec((tm, tn), lambda j, i: (i, j)),
        out_shape=jax.ShapeDtypeStruct((m, d), F32),
        scratch_shapes=[pltpu.VMEM((dff, tn), BF16)],
        compiler_params=_params("arbitrary", "arbitrary"),
        name="ffn_down",
    )(act, w_down, h)


def _ple_kernel(r_ref, p_ref, wg_ref, wp_ref, h_ref, o_ref, wg_bf, wp_bf):
    @pl.when(pl.program_id(1) == 0)
    def _():
        wg_bf[...] = wg_ref[...].astype(BF16)
        wp_bf[...] = wp_ref[...].astype(BF16)

    gate = _sigmoid(_dot(r_ref[...], wg_bf[...]))
    emb = _dot(p_ref[...].astype(BF16), wp_bf[...])
    o_ref[...] = h_ref[...] + emb * gate


def _ple(r, p, w_gate, w_proj, h, tm):
    m, d = h.shape
    pd = p.shape[1]
    tn = COL_TILE
    return pl.pallas_call(
        _ple_kernel,
        grid=(d // tn, m // tm),
        in_specs=[pl.BlockSpec((tm, d), lambda j, i: (i, 0)),
                  pl.BlockSpec((tm, pd), lambda j, i: (i, 0)),
                  pl.BlockSpec((d, tn), lambda j, i: (0, j)),
                  pl.BlockSpec((pd, tn), lambda j, i: (0, j)),
                  pl.BlockSpec((tm, tn), lambda j, i: (i, j))],
        out_specs=pl.BlockSpec((tm, tn), lambda j, i: (i, j)),
        out_shape=jax.ShapeDtypeStruct((m, d), F32),
        scratch_shapes=[pltpu.VMEM((d, tn), BF16), pltpu.VMEM((pd, tn), BF16)],
        compiler_params=_params("arbitrary", "arbitrary"),
        name="ple",
    )(r, p, w_gate, w_proj, h)


def _layer(x, p, sb_fn, gdn_conv_state, gdn_rec_state, ffn_conv_state, weights, final_norm):
    (attn_norm, w_in, sb_out_norm, gdn_conv_w, gdn_a_log, gdn_dt_bias, gdn_out_norm, w_out,
     ffn_norm, w_ffn_gate, w_ffn_up, ffn_conv_w, w_ffn_down, ple_norm, w_ple_gate,
     w_ple_proj) = weights
    b, t, d = x.shape
    m = b * t
    tm = min(ROW_TILE, m)
    x2 = x.reshape(m, d)

    a = _rmsnorm(x2, attn_norm, BF16, tm)
    proj = _proj(a, w_in, tm)
    ab, abt = _gate_proj(a, w_in[:, w_in.shape[1] - LANES:], tm)
    mix_sb = sb_fn(proj)

    conv_in = proj[:, O_GDN_QKV:O_GDN_Z].reshape(b, t, GDN_CONV_CH)
    if t % GDN_C == 0:
        mix_gdn, rec = _gdn(proj, O_GDN_QKV // GDN_CONV_CH, O_GDN_Z // GDN_WIDTH, ab, abt,
                            gdn_conv_state, gdn_rec_state, gdn_conv_w, gdn_a_log, gdn_dt_bias,
                            gdn_out_norm, b, t, t)
    else:
        t_pad = -(-t // GDN_C) * GDN_C
        pad_rows = lambda arr: jnp.pad(arr.reshape(b, t, -1), ((0, 0), (0, t_pad - t), (0, 0))
                                       ).reshape(b * t_pad, -1)
        src = pad_rows(proj[:, O_GDN_QKV:])
        abt_pad = jnp.pad(abt.reshape(LANES, b, t), ((0, 0), (0, 0), (0, t_pad - t))
                          ).reshape(LANES, b * t_pad)
        mix_gdn, rec = _gdn(src, 0, GDN_CONV_CH // GDN_WIDTH, pad_rows(ab), abt_pad,
                            gdn_conv_state, gdn_rec_state, gdn_conv_w, gdn_a_log, gdn_dt_bias,
                            gdn_out_norm, b, t_pad, t)
        mix_gdn = mix_gdn.reshape(b, t_pad, GDN_WIDTH)[:, :t].reshape(m, GDN_WIDTH)
    new_gdn_conv = jnp.concatenate([gdn_conv_state, conv_in], axis=1)[:, t:]

    h = _outproj(mix_sb, mix_gdn, w_out, x2, tm)
    f = _rmsnorm(h, ffn_norm, BF16, tm)
    act, new_ffn_conv = _ffn_up(f, w_ffn_gate, w_ffn_up, ffn_conv_w, ffn_conv_state, tm, t)
    h = _ffn_down(act, w_ffn_down, h, tm)
    r = _rmsnorm(h, ple_norm, BF16, tm)
    h = _ple(r, p.reshape(m, -1), w_ple_gate, w_ple_proj, h, tm)
    y = _rmsnorm(h, final_norm, F32, tm)

    sb_k = proj[:, SB_WIDTH:2 * SB_WIDTH].reshape(b, t, SB_HEADS, HEAD_DIM)
    sb_v = proj[:, 2 * SB_WIDTH:3 * SB_WIDTH].reshape(b, t, SB_HEADS, HEAD_DIM)
    return y.reshape(b, t, d), sb_k, sb_v, new_gdn_conv, rec, new_ffn_conv


def kernel(x_prompt, x_sample, cache_sb_k, cache_sb_v, page_table, state_gdn_conv, state_gdn_rec, state_ffn_conv, p_prompt, p_sample, attn_norm, w_in, sb_logit_bias, sb_out_norm, gdn_conv_w, gdn_a_log, gdn_dt_bias, gdn_out_norm, w_out, ffn_norm, w_ffn_gate, w_ffn_up, ffn_conv_w, w_ffn_down, ple_norm, w_ple_gate, w_ple_proj, final_norm):
    assert attn_norm.shape[0] == 1, "single-layer step"
    assert x_sample.shape[1] == 1, "decode group carries one token per sequence"
    bp, tp, _ = x_prompt.shape
    bs = x_sample.shape[0]
    weights = (attn_norm[0], w_in[0], sb_out_norm[0], gdn_conv_w[0], gdn_a_log[0], gdn_dt_bias[0],
               gdn_out_norm[0], w_out[0], ffn_norm[0], w_ffn_gate[0], w_ffn_up[0], ffn_conv_w[0],
               w_ffn_down[0], ple_norm[0], w_ple_gate[0], w_ple_proj[0])
    bias = sb_logit_bias[0]
    out_norm = sb_out_norm[0]

    zeros = lambda *shape: jnp.zeros(shape, x_prompt.dtype)
    out_p = _layer(
        x_prompt, p_prompt[0],
        lambda proj: _sb_prompt(proj, bias, out_norm, bp, tp),
        zeros(bp, GDN_CONV - 1, GDN_CONV_CH), zeros(bp, GDN_HEADS, HEAD_DIM, HEAD_DIM),
        zeros(bp, FFN_CONV - 1, w_ffn_gate.shape[2]), weights, final_norm)
    out_s = _layer(
        x_sample, p_sample[0],
        lambda proj: _sb_decode(proj, cache_sb_k[0], cache_sb_v[0], page_table, bias, out_norm),
        state_gdn_conv[0], state_gdn_rec[0], state_ffn_conv[0], weights, final_norm)

    (y_p, *st_p), (y_s, *st_s) = out_p, out_s
    return (y_p, y_s, *[s[None] for s in st_p], *[s[None] for s in st_s])
```

```python
import functools

import jax
import jax.numpy as jnp
from jax import lax
from jax.experimental import pallas as pl
from jax.experimental.pallas import tpu as pltpu

F32 = jnp.float32
BF16 = jnp.bfloat16

EPS = 1e-6
HEAD_DIM = 128
SB_HEADS = 8
GDN_HEADS = 8
SB_WIDTH = SB_HEADS * HEAD_DIM
GDN_WIDTH = GDN_HEADS * HEAD_DIM
GDN_CONV = 4
GDN_CONV_CH = 3 * GDN_WIDTH
FFN_CONV = 3
SB_SCALE = HEAD_DIM ** -0.5
O_GDN_QKV = 3 * SB_WIDTH
O_GDN_Z = O_GDN_QKV + GDN_CONV_CH
PROJ_MAIN = O_GDN_Z + GDN_WIDTH

LANES = 128
SUBLANES = 8
VMEM_LIMIT = 56 * 1024 * 1024
COL_TILE = 512
ROW_TILE = 512
SB_TQ = 512
SB_PAGES_PER_STEP = 8
GDN_C = 128
A_LANE = LANES - 2 * GDN_HEADS
B_LANE = LANES - GDN_HEADS

_NT = (((1,), (1,)), ((), ()))


def _params(*sem):
    return pltpu.CompilerParams(dimension_semantics=sem, vmem_limit_bytes=VMEM_LIMIT)


def _dot(a, b):
    return jnp.dot(a, b, preferred_element_type=F32)


def _dot_nt(a, b):
    return lax.dot_general(a, b, _NT, preferred_element_type=F32)


def _softplus(x):
    return jnp.maximum(x, 0.0) + jnp.log1p(jnp.exp(-jnp.abs(x)))


def _softplus_fast(x):
    return jnp.maximum(x, 0.0) + jnp.log(1.0 + jnp.exp(-jnp.abs(x)))


def _sigmoid(x):
    return 1.0 / (1.0 + jnp.exp(-x))


def _split2(x):
    hi = x.astype(BF16)
    lo = (x - hi.astype(F32)).astype(BF16)
    return hi, lo


def _split3(x):
    h1 = x.astype(BF16)
    r1 = x - h1.astype(F32)
    h2 = r1.astype(BF16)
    h3 = (r1 - h2.astype(F32)).astype(BF16)
    return h1, h2, h3


def _rmsnorm_kernel(x_ref, w_ref, o_ref):
    x = x_ref[...]
    y = x * lax.rsqrt(jnp.mean(x * x, axis=-1, keepdims=True) + EPS)
    o_ref[...] = (y * w_ref[...]).astype(o_ref.dtype)


def _rmsnorm(x, w, out_dtype, tm):
    m, d = x.shape
    return pl.pallas_call(
        _rmsnorm_kernel,
        grid=(m // tm,),
        in_specs=[pl.BlockSpec((tm, d), lambda i: (i, 0)),
                  pl.BlockSpec((1, d), lambda i: (0, 0))],
        out_specs=pl.BlockSpec((tm, d), lambda i: (i, 0)),
        out_shape=jax.ShapeDtypeStruct((m, d), out_dtype),
        compiler_params=_params("parallel"),
        name="rmsnorm",
    )(x, w.reshape(1, d))


def _proj_kernel(a_ref, w_ref, o_ref, wbf_ref):
    @pl.when(pl.program_id(1) == 0)
    def _():
        wbf_ref[...] = w_ref[...].astype(BF16)

    o_ref[...] = _dot(a_ref[...], wbf_ref[...])


def _proj(a, w_in, tm):
    m, d = a.shape
    tn = COL_TILE
    return pl.pallas_call(
        _proj_kernel,
        grid=(PROJ_MAIN // tn, m // tm),
        in_specs=[pl.BlockSpec((tm, d), lambda j, i: (i, 0)),
                  pl.BlockSpec((d, tn), lambda j, i: (0, j))],
        out_specs=pl.BlockSpec((tm, tn), lambda j, i: (i, j)),
        out_shape=jax.ShapeDtypeStruct((m, PROJ_MAIN), F32),
        scratch_shapes=[pltpu.VMEM((d, tn), BF16)],
        compiler_params=_params("arbitrary", "arbitrary"),
        name="proj",
    )(a, w_in)


def _gate_proj_kernel(a_ref, w_ref, wt_ref, ab_ref, abt_ref):
    a = a_ref[...]
    ab_ref[...] = _dot(a, w_ref[...])
    abt_ref[...] = _dot_nt(wt_ref[...], a)


def _gate_proj(a, w_tail, tm):
    m, d = a.shape
    w = w_tail.astype(BF16)
    return pl.pallas_call(
        _gate_proj_kernel,
        grid=(m // tm,),
        in_specs=[pl.BlockSpec((tm, d), lambda i: (i, 0)),
                  pl.BlockSpec((d, LANES), lambda i: (0, 0)),
                  pl.BlockSpec((LANES, d), lambda i: (0, 0))],
        out_specs=[pl.BlockSpec((tm, LANES), lambda i: (i, 0)),
                   pl.BlockSpec((LANES, tm), lambda i: (0, i))],
        out_shape=[jax.ShapeDtypeStruct((m, LANES), F32),
                   jax.ShapeDtypeStruct((LANES, m), F32)],
        compiler_params=_params("parallel"),
        name="gate_proj",
    )(a, w, w.T)


def _cumsum_weights():
    j = lax.broadcasted_iota(jnp.int32, (LANES, LANES), 0)
    s = lax.broadcasted_iota(jnp.int32, (LANES, LANES), 1)
    incl = (j >= s).astype(BF16)
    half = jnp.concatenate([incl, jnp.ones((LANES, LANES), BF16)], axis=1)
    return jnp.concatenate([half, half], axis=0)


def _sb_chunk(q, kc, vc, z_bias, w2, r, diag):
    n = kc.shape[0] // LANES
    z = _dot_nt(q, kc) + z_bias
    blocks = []
    for c in range(n):
        row0 = c * LANES if diag else 0
        zc = z[row0:, c * LANES:(c + 1) * LANES]
        sp = _softplus_fast(zc)
        valid = None
        if diag:
            valid = (lax.broadcasted_iota(jnp.int32, zc.shape, 1)
                     < lax.broadcasted_iota(jnp.int32, zc.shape, 0))
            sp = jnp.where(valid, sp, 0.0)
        hi, lo = _split2(sp)
        blocks.append((row0, zc, valid, jnp.concatenate([hi, lo], axis=1)))
    res = _dot(jnp.concatenate([blk[3] for blk in blocks], axis=0), w2)
    ends = []
    for blk in blocks:
        ends.append((ends[-1] if ends else 0) + blk[1].shape[0])
    a_parts = [None] * n
    for c in reversed(range(n)):
        row0, zc, valid, _ = blocks[c]
        rc = res[ends[c] - zc.shape[0]:ends[c]]
        a = jnp.exp(zc - rc[:, :LANES] - r[row0:])
        if diag:
            a = jnp.where(valid, a, 0.0)
        a = a.astype(BF16)
        r_new = r[row0:] + rc[:, LANES:]
        if row0:
            a = jnp.concatenate([jnp.zeros((row0, LANES), BF16), a], axis=0)
            r_new = jnp.concatenate([r[:row0], r_new], axis=0)
        a_parts[c] = a
        r = r_new
    return _dot(jnp.concatenate(a_parts, axis=1), vc), r


def _sb_prompt_kernel(c_ref, q_ref, k_ref, v_ref, w2_ref, nw_ref, o_ref, kbf, vbf, acc, run):
    h = pl.program_id(1)
    i = pl.program_id(2)

    @pl.when(i == 0)
    def _():
        kbf[...] = k_ref[...].astype(BF16)
        vbf[...] = v_ref[...].astype(BF16)

    z_bias = c_ref[h]
    w2 = w2_ref[...]
    q = (q_ref[...] * SB_SCALE).astype(BF16)

    def chunk(start, r, diag):
        ks = pl.ds(pl.multiple_of(start, SB_TQ), SB_TQ)
        return _sb_chunk(q, kbf[ks, :], vbf[ks, :], z_bias, w2, r, diag)

    d, r = chunk(i * SB_TQ, jnp.zeros(run.shape, F32), True)
    acc[...] = d
    run[...] = r

    def body(j, carry):
        d, r = chunk((i - 1 - j) * SB_TQ, run[...], False)
        acc[...] += d
        run[...] = r
        return carry

    lax.fori_loop(0, i, body, 0)

    o = acc[...]
    y = o * lax.rsqrt(jnp.mean(o * o, axis=-1, keepdims=True) + EPS) * nw_ref[...]
    o_ref[...] = y.astype(o_ref.dtype)


def _sb_prompt(proj, logit_bias, out_norm, b, t):
    nq = t // SB_TQ
    blk = lambda rows, fn: pl.BlockSpec((rows, HEAD_DIM), fn)
    return pl.pallas_call(
        _sb_prompt_kernel,
        grid=(b, SB_HEADS, nq),
        in_specs=[pl.BlockSpec(memory_space=pltpu.SMEM),
                  blk(SB_TQ, lambda bi, h, i: (bi * nq + i, h)),
                  blk(t, lambda bi, h, i: (bi, SB_HEADS + h)),
                  blk(t, lambda bi, h, i: (bi, 2 * SB_HEADS + h)),
                  pl.BlockSpec((2 * LANES, 2 * LANES), lambda bi, h, i: (0, 0)),
                  pl.BlockSpec((1, HEAD_DIM), lambda bi, h, i: (0, 0))],
        out_specs=blk(SB_TQ, lambda bi, h, i: (bi * nq + i, h)),
        out_shape=jax.ShapeDtypeStruct((b * t, SB_WIDTH), BF16),
        scratch_shapes=[pltpu.VMEM((t, HEAD_DIM), BF16), pltpu.VMEM((t, HEAD_DIM), BF16),
                        pltpu.VMEM((SB_TQ, HEAD_DIM), F32), pltpu.VMEM((SB_TQ, LANES), F32)],
        compiler_params=_params("arbitrary", "arbitrary", "arbitrary"),
        name="sb_prompt",
    )(logit_bias, proj, proj, proj, _cumsum_weights(), out_norm.reshape(1, HEAD_DIM))


def _sb_decode_kernel(pt_ref, q_ref, kn_ref, vn_ref, *rest, past_len):
    k_refs = rest[:SB_PAGES_PER_STEP]
    v_refs = rest[SB_PAGES_PER_STEP:2 * SB_PAGES_PER_STEP]
    c_ref, nw_ref, o_ref, acc, run = rest[2 * SB_PAGES_PER_STEP:]
    p = pl.program_id(1)
    page = k_refs[0].shape[1]
    flat = SB_PAGES_PER_STEP * page * SB_HEADS
    lane = lax.broadcasted_iota(jnp.int32, (SB_HEADS, flat), 1)
    own = (lane % SB_HEADS) == lax.broadcasted_iota(jnp.int32, (SB_HEADS, flat), 0)
    z_bias = c_ref[...][:, :1]
    qb = (q_ref[0] * SB_SCALE).astype(BF16)

    @pl.when(p == 0)
    def _():
        kn = kn_ref[0].astype(BF16).astype(F32)
        z = jnp.sum(qb.astype(F32) * kn, axis=-1, keepdims=True) + z_bias
        valid = jnp.full(z.shape, past_len, jnp.int32) < past_len
        sp = jnp.where(valid, _softplus(z), 0.0)
        a = jnp.where(valid, jnp.exp(z - sp), 0.0)
        acc[...] = a.astype(BF16).astype(F32) * vn_ref[0].astype(BF16).astype(F32)
        run[...] = jnp.broadcast_to(sp, run.shape)

    flatten = lambda ref: ref[0].reshape(page * SB_HEADS, HEAD_DIM).astype(BF16)
    kp = jnp.concatenate([flatten(ref) for ref in reversed(k_refs)], axis=0)
    vp = jnp.concatenate([flatten(ref) for ref in reversed(v_refs)], axis=0)
    z = _dot_nt(qb, kp) + z_bias
    sp = jnp.where(own, _softplus(z), 0.0)
    s = sp
    step = SB_HEADS
    while step < flat:
        s = s + jnp.where(lane + step < flat, pltpu.roll(s, flat - step, axis=1), 0.0)
        step *= 2
    r = run[...]
    a = jnp.where(own, jnp.exp(z - s - r[:, :1]), 0.0)
    run[...] = r + jnp.sum(sp, axis=1, keepdims=True)
    acc[...] += _dot(a.astype(BF16), vp)

    @pl.when(p == pl.num_programs(1) - 1)
    def _():
        o = acc[...]
        y = o * lax.rsqrt(jnp.mean(o * o, axis=-1, keepdims=True) + EPS) * nw_ref[...]
        o_ref[0] = y.astype(o_ref.dtype)


def _sb_decode(proj, cache_k, cache_v, page_table, logit_bias, out_norm):
    b = proj.shape[0]
    _, page, _, _ = cache_k.shape
    n_pages = page_table.shape[1]
    pps = SB_PAGES_PER_STEP
    assert n_pages % pps == 0
    proj3 = proj.reshape(b, PROJ_MAIN // HEAD_DIM, HEAD_DIM)
    row = lambda blk: pl.BlockSpec((1, SB_HEADS, HEAD_DIM), lambda bi, p, pt: (bi, blk, 0))
    pg = lambda s: pl.BlockSpec(
        (1, page, SB_HEADS, HEAD_DIM),
        lambda bi, p, pt: (pt[bi * n_pages + n_pages - 1 - (p * pps + s)], 0, 0, 0))
    pages = [pg(s) for s in range(pps)]
    const = lambda shape: pl.BlockSpec(shape, lambda bi, p, pt: (0,) * len(shape))
    grid_spec = pltpu.PrefetchScalarGridSpec(
        num_scalar_prefetch=1,
        grid=(b, n_pages // pps),
        in_specs=[row(0), row(1), row(2), *pages, *pages,
                  const((SB_HEADS, LANES)), const((1, HEAD_DIM))],
        out_specs=pl.BlockSpec((1, SB_HEADS, HEAD_DIM), lambda bi, p, pt: (bi, 0, 0)),
        scratch_shapes=[pltpu.VMEM((SB_HEADS, HEAD_DIM), F32),
                        pltpu.VMEM((SB_HEADS, LANES), F32)],
    )
    out = pl.pallas_call(
        functools.partial(_sb_decode_kernel, past_len=n_pages * page),
        grid_spec=grid_spec,
        out_shape=jax.ShapeDtypeStruct((b, SB_HEADS, HEAD_DIM), BF16),
        compiler_params=_params("arbitrary", "arbitrary"),
        name="sb_decode",
    )(page_table.reshape(-1), proj3, proj3, proj3, *([cache_k] * pps), *([cache_v] * pps),
      jnp.broadcast_to(logit_bias[:, None], (SB_HEADS, LANES)), out_norm.reshape(1, HEAD_DIM))
    return out.reshape(b, SB_WIDTH)


def _shift_rows(x, prev, k):
    xr = pltpu.roll(x, k, axis=0)
    row = lax.broadcasted_iota(jnp.int32, prev.shape, 0)
    top = jnp.where(row < k, pltpu.roll(prev, k, axis=0), xr[:SUBLANES])
    return jnp.concatenate([top, xr[SUBLANES:]], axis=0)


def _gdn_kernel(cin_ref, z_ref, ab_ref, abt_ref, cst_ref, s0_ref, cw_ref, prow_ref, pcol_ref,
                nw_ref, o_ref, sfin_ref, s_scr, hist_scr, *, t_valid, t_total):
    c_len = GDN_C
    t = pl.program_id(1)

    @pl.when(t == 0)
    def _():
        s_scr[...] = s0_ref[0]
        hist_scr[...] = cst_ref[0]

    x = cin_ref[...]
    hist = hist_scr[...]
    cw = cw_ref[...]
    y = x * cw[3:4]
    for kk in range(1, GDN_CONV):
        y = y + _shift_rows(x, hist, kk) * cw[GDN_CONV - 1 - kk:GDN_CONV - kk]
    hist_scr[...] = x[c_len - SUBLANES:]
    conv = y * _sigmoid(y)

    ri = lax.broadcasted_iota(jnp.int32, (c_len, c_len), 0)
    ci = lax.broadcasted_iota(jnp.int32, (c_len, c_len), 1)
    tri = ri >= ci
    strict = ri > ci
    eye = (ri == ci).astype(F32)
    eye_bf = eye.astype(BF16)
    ltri = tri.astype(BF16)
    utri = (ri <= ci).astype(BF16)

    ab = ab_ref[...]
    g_col = -jnp.exp(prow_ref[0:1]) * _softplus(ab + prow_ref[1:2])
    beta_col = _sigmoid(ab)
    abt = abt_ref[0]
    g_row = (-jnp.exp(pcol_ref[0:GDN_HEADS]) *
             _softplus(abt[0:GDN_HEADS] + pcol_ref[GDN_HEADS:2 * GDN_HEADS]))
    if t_valid < t_total:
        pos_c = t * c_len + lax.broadcasted_iota(jnp.int32, (c_len, LANES), 0)
        g_col = jnp.where(pos_c < t_valid, g_col, 0.0)
        beta_col = jnp.where(pos_c < t_valid, beta_col, 0.0)
        pos_r = t * c_len + lax.broadcasted_iota(jnp.int32, (GDN_HEADS, c_len), 1)
        g_row = jnp.where(pos_r < t_valid, g_row, 0.0)
    gc_col = sum(_dot(ltri, part) for part in _split3(g_col))
    gc_row = sum(_dot(part, utri) for part in _split3(g_row))

    levels = []
    m = 1
    while m < c_len:
        lm = m.bit_length() - 1
        same = ((ri ^ ci) >> (lm + 1)) == 0
        lower = ((ri >> lm) & 1) > ((ci >> lm) & 1)
        levels.append(jnp.where(same, jnp.where(lower, 1.0, 0.0), 0.0))
        m *= 2

    heads = range(GDN_HEADS)
    head_cols = lambda base, h: slice(base + h * HEAD_DIM, base + (h + 1) * HEAD_DIM)
    qn, kn, kn_bf, kb, vb, decay, egc, gcc = [], [], [], [], [], [], [], []
    for h in heads:
        q = conv[:, head_cols(0, h)]
        k = conv[:, head_cols(GDN_WIDTH, h)]
        v = conv[:, head_cols(2 * GDN_WIDTH, h)]
        qn.append(q * lax.rsqrt(jnp.sum(q * q, axis=-1, keepdims=True) + 1e-6) * (HEAD_DIM ** -0.5))
        kn.append(k * lax.rsqrt(jnp.sum(k * k, axis=-1, keepdims=True) + 1e-6))
        kn_bf.append(kn[h].astype(BF16))
        gcc.append(gc_col[:, A_LANE + h:A_LANE + h + 1])
        beta = beta_col[:, B_LANE + h:B_LANE + h + 1]
        gcr = gc_row[h:h + 1, :]
        kb.append(kn[h] * beta)
        vb.append(v * beta)
        decay.append(jnp.where(tri, jnp.exp(jnp.where(tri, gcc[h] - gcr, 0.0)), 0.0))
        egc.append(jnp.exp(gcc[h]))
    low = [jnp.where(strict, _dot_nt(kb[h].astype(BF16), kn_bf[h]) * decay[h], 0.0) for h in heads]
    qk = [(_dot_nt(qn[h].astype(BF16), kn_bf[h]) * decay[h]).astype(BF16) for h in heads]

    xinv = [eye - low[h] * levels[0] for h in heads]
    for lvl in levels[1:]:
        xb = [xinv[h].astype(BF16) for h in heads]
        y1 = [_dot(xb[h], (low[h] * lvl).astype(BF16)).astype(BF16) for h in heads]
        xinv = [xinv[h] - _dot(y1[h], xb[h]) for h in heads]

    sol = []
    for h in heads:
        rhs = jnp.concatenate([vb[h], kb[h] * egc[h]], axis=1)
        sol.append(rhs + _dot((xinv[h] - eye).astype(BF16), rhs.astype(BF16)))
    s_old = [s_scr[h] for h in heads]
    ws_qs = [_dot(jnp.concatenate([sol[h][:, HEAD_DIM:], qn[h] * egc[h]], axis=0).astype(BF16),
                  s_old[h].astype(BF16)) for h in heads]
    v_new = [(sol[h][:, :HEAD_DIM] - ws_qs[h][:c_len]).astype(BF16) for h in heads]
    out = [ws_qs[h][c_len:] + _dot(qk[h], v_new[h]) for h in heads]
    nw = nw_ref[...]
    for h in heads:
        g_last = gcc[h][c_len - 1:c_len, :]
        kd = (kn[h] * jnp.exp(g_last - gcc[h])).astype(BF16)
        kd_t = _dot_nt(eye_bf, kd).astype(BF16)
        s_scr[h] = s_old[h] * jnp.exp(g_last) + _dot(kd_t, v_new[h])
    for h in heads:
        zg = z_ref[:, head_cols(0, h)]
        o = out[h]
        on = o * lax.rsqrt(jnp.mean(o * o, axis=-1, keepdims=True) + EPS) * nw
        o_ref[:, head_cols(0, h)] = (on * (zg * _sigmoid(zg))).astype(o_ref.dtype)

    @pl.when(t == pl.num_programs(1) - 1)
    def _():
        sfin_ref[0] = s_scr[...]


def _gdn(src, cin_blk, z_blk, ab, abt, conv_state, rec_state, conv_w, a_log, dt_bias, out_norm,
         b, t_pad, t_valid):
    c_len = GDN_C
    nt = t_pad // c_len
    cst = jnp.pad(conv_state, ((0, 0), (SUBLANES - (GDN_CONV - 1), 0), (0, 0)))
    cw = jnp.pad(conv_w, ((0, SUBLANES - GDN_CONV), (0, 0)))
    prow = jnp.zeros((SUBLANES, LANES), F32)
    prow = prow.at[0, A_LANE:B_LANE].set(a_log).at[1, A_LANE:B_LANE].set(dt_bias)
    pcol = jnp.broadcast_to(jnp.concatenate([a_log, dt_bias])[:, None], (2 * GDN_HEADS, LANES))
    abt3 = abt[A_LANE:].reshape(2 * GDN_HEADS, b * nt, c_len).transpose(1, 0, 2)
    full = lambda shape: pl.BlockSpec(shape, lambda bi, ti: (0,) * len(shape))
    return pl.pallas_call(
        functools.partial(_gdn_kernel, t_valid=t_valid, t_total=t_pad),
        grid=(b, nt),
        in_specs=[pl.BlockSpec((c_len, GDN_CONV_CH), lambda bi, ti: (bi * nt + ti, cin_blk)),
                  pl.BlockSpec((c_len, GDN_WIDTH), lambda bi, ti: (bi * nt + ti, z_blk)),
                  pl.BlockSpec((c_len, LANES), lambda bi, ti: (bi * nt + ti, 0)),
                  pl.BlockSpec((1, 2 * GDN_HEADS, c_len), lambda bi, ti: (bi * nt + ti, 0, 0)),
                  pl.BlockSpec((1, SUBLANES, GDN_CONV_CH), lambda bi, ti: (bi, 0, 0)),
                  pl.BlockSpec((1, GDN_HEADS, HEAD_DIM, HEAD_DIM), lambda bi, ti: (bi, 0, 0, 0)),
                  full((SUBLANES, GDN_CONV_CH)), full((SUBLANES, LANES)),
                  full((2 * GDN_HEADS, LANES)), full((1, HEAD_DIM))],
        out_specs=[pl.BlockSpec((c_len, GDN_WIDTH), lambda bi, ti: (bi * nt + ti, 0)),
                   pl.BlockSpec((1, GDN_HEADS, HEAD_DIM, HEAD_DIM), lambda bi, ti: (bi, 0, 0, 0))],
        out_shape=[jax.ShapeDtypeStruct((b * t_pad, GDN_WIDTH), BF16),
                   jax.ShapeDtypeStruct((b, GDN_HEADS, HEAD_DIM, HEAD_DIM), F32)],
        scratch_shapes=[pltpu.VMEM((GDN_HEADS, HEAD_DIM, HEAD_DIM), F32),
                        pltpu.VMEM((SUBLANES, GDN_CONV_CH), F32)],
        compiler_params=_params("arbitrary", "arbitrary"),
        name="gdn",
    )(src, src, ab, abt3, cst, rec_state, cw, prow, pcol, out_norm.reshape(1, HEAD_DIM))


def _outproj_kernel(ms_ref, mg_ref, w0_ref, w1_ref, x_ref, o_ref, w0_bf, w1_bf):
    @pl.when(pl.program_id(1) == 0)
    def _():
        w0_bf[...] = w0_ref[...].astype(BF16)
        w1_bf[...] = w1_ref[...].astype(BF16)

    o_ref[...] = x_ref[...] + _dot(ms_ref[...], w0_bf[...]) + _dot(mg_ref[...], w1_bf[...])


def _outproj(mix_sb, mix_gdn, w_out, x, tm):
    m, d = x.shape
    tn = COL_TILE
    return pl.pallas_call(
        _outproj_kernel,
        grid=(d // tn, m // tm),
        in_specs=[pl.BlockSpec((tm, SB_WIDTH), lambda j, i: (i, 0)),
                  pl.BlockSpec((tm, GDN_WIDTH), lambda j, i: (i, 0)),
                  pl.BlockSpec((SB_WIDTH, tn), lambda j, i: (0, j)),
                  pl.BlockSpec((GDN_WIDTH, tn), lambda j, i: (1, j)),
                  pl.BlockSpec((tm, tn), lambda j, i: (i, j))],
        out_specs=pl.BlockSpec((tm, tn), lambda j, i: (i, j)),
        out_shape=jax.ShapeDtypeStruct((m, d), F32),
        scratch_shapes=[pltpu.VMEM((SB_WIDTH, tn), BF16), pltpu.VMEM((GDN_WIDTH, tn), BF16)],
        compiler_params=_params("arbitrary", "arbitrary"),
        name="outproj",
    )(mix_sb, mix_gdn, w_out, w_out, x)


def _ffn_up_kernel(f_ref, wg_ref, wu_ref, cw_ref, hist_ref, act_ref, st_ref, wg_bf, wu_bf, carry,
                   *, tiles_per_seq, single_token):
    i = pl.program_id(1)

    @pl.when(i == 0)
    def _():
        wg_bf[...] = wg_ref[...].astype(BF16)
        wu_bf[...] = wu_ref[...].astype(BF16)

    f = f_ref[...]
    g = _dot(f, wg_bf[...])
    u = _dot(f, wu_bf[...])
    cw = cw_ref[...]
    if single_token:
        h0 = hist_ref[0]
        h1 = hist_ref[1]
        gate = h0 * cw[0:1] + h1 * cw[1:2] + g * cw[2:3]
        st_ref[0] = h1
        st_ref[1] = g
    else:
        @pl.when(i % tiles_per_seq == 0)
        def _():
            carry[...] = hist_ref[0]

        prev = carry[...]
        gate = (_shift_rows(g, prev, 2) * cw[0:1] + _shift_rows(g, prev, 1) * cw[1:2]
                + g * cw[2:3])
        last = g[g.shape[0] - SUBLANES:]
        carry[...] = last
        st_ref[0] = last
    act_ref[...] = (gate * _sigmoid(gate) * u).astype(act_ref.dtype)


def _ffn_up(f, w_gate, w_up, conv_w, state, tm, seq_len):
    m, d = f.shape
    dff = w_gate.shape[1]
    tn = COL_TILE
    nj = pl.cdiv(dff, tn)
    cw = jnp.pad(conv_w, ((0, SUBLANES - FFN_CONV), (0, 0)))
    single = seq_len == 1
    nb = m // seq_len
    if single:
        assert tm == m
        hist = state.transpose(1, 0, 2)
        hist_spec = pl.BlockSpec((FFN_CONV - 1, m, tn), lambda j, i: (0, 0, j))
        st_spec = pl.BlockSpec((FFN_CONV - 1, m, tn), lambda j, i: (0, 0, j))
        st_shape = jax.ShapeDtypeStruct((FFN_CONV - 1, m, dff), F32)
        tiles_per_seq = 1
    else:
        tiles_per_seq = seq_len // tm
        hist = jnp.pad(state, ((0, 0), (SUBLANES - (FFN_CONV - 1), 0), (0, 0)))
        hist_spec = pl.BlockSpec((1, SUBLANES, tn), lambda j, i: (i // tiles_per_seq, 0, j))
        st_spec = pl.BlockSpec((1, SUBLANES, tn), lambda j, i: (i // tiles_per_seq, 0, j))
        st_shape = jax.ShapeDtypeStruct((nb, SUBLANES, dff), F32)
    act, st = pl.pallas_call(
        functools.partial(_ffn_up_kernel, tiles_per_seq=tiles_per_seq, single_token=single),
        grid=(nj, m // tm),
        in_specs=[pl.BlockSpec((tm, d), lambda j, i: (i, 0)),
                  pl.BlockSpec((d, tn), lambda j, i: (0, j)),
                  pl.BlockSpec((d, tn), lambda j, i: (0, j)),
                  pl.BlockSpec((SUBLANES, tn), lambda j, i: (0, j)),
                  hist_spec],
        out_specs=[pl.BlockSpec((tm, tn), lambda j, i: (i, j)), st_spec],
        out_shape=[jax.ShapeDtypeStruct((m, dff), BF16), st_shape],
        scratch_shapes=[pltpu.VMEM((d, tn), BF16), pltpu.VMEM((d, tn), BF16),
                        pltpu.VMEM((SUBLANES, tn), F32)],
        compiler_params=_params("arbitrary", "arbitrary"),
        name="ffn_up",
    )(f, w_gate, w_up, cw, hist)
    if single:
        return act, st.transpose(1, 0, 2)
    return act, st[:, SUBLANES - (FFN_CONV - 1):]


def _ffn_down_kernel(act_ref, w_ref, h_ref, o_ref, w_bf):
    @pl.when(pl.program_id(1) == 0)
    def _():
        w_bf[...] = w_ref[...].astype(BF16)

    o_ref[...] = h_ref[...] + _dot(act_ref[...], w_bf[...])


def _ffn_down(act, w_down, h, tm):
    m, dff = act.shape
    d = h.shape[1]
    tn = COL_TILE
    return pl.pallas_call(
        _ffn_down_kernel,
        grid=(d // tn, m // tm),
        in_specs=[pl.BlockSpec((tm, dff), lambda j, i: (i, 0)),
                  pl.BlockSpec((dff, tn), lambda j, i: (0, j)),
                  pl.BlockSpec((tm, tn), lambda j, i: (i, j))],
        out_specs=pl.BlockSpec((tm, tn), lambda j, i: (i, j)),
        out_shape=jax.ShapeDtypeStruct((m, d), F32),
        scratch_shapes=[pltpu.VMEM((dff, tn), BF16)],
        compiler_params=_params("arbitrary", "arbitrary"),
        name="ffn_down",
    )(act, w_down, h)


def _ple_kernel(r_ref, p_ref, wg_ref, wp_ref, h_ref, o_ref, wg_bf, wp_bf):
    @pl.when(pl.program_id(1) == 0)
    def _():
        wg_bf[...] = wg_ref[...].astype(BF16)
        wp_bf[...] = wp_ref[...].astype(BF16)

    gate = _sigmoid(_dot(r_ref[...], wg_bf[...]))
    emb = _dot(p_ref[...].astype(BF16), wp_bf[...])
    o_ref[...] = h_ref[...] + emb * gate


def _ple(r, p, w_gate, w_proj, h, tm):
    m, d = h.shape
    pd = p.shape[1]
    tn = COL_TILE
    return pl.pallas_call(
        _ple_kernel,
        grid=(d // tn, m // tm),
        in_specs=[pl.BlockSpec((tm, d), lambda j, i: (i, 0)),
                  pl.BlockSpec((tm, pd), lambda j, i: (i, 0)),
                  pl.BlockSpec((d, tn), lambda j, i: (0, j)),
                  pl.BlockSpec((pd, tn), lambda j, i: (0, j)),
                  pl.BlockSpec((tm, tn), lambda j, i: (i, j))],
        out_specs=pl.BlockSpec((tm, tn), lambda j, i: (i, j)),
        out_shape=jax.ShapeDtypeStruct((m, d), F32),
        scratch_shapes=[pltpu.VMEM((d, tn), BF16), pltpu.VMEM((pd, tn), BF16)],
        compiler_params=_params("arbitrary", "arbitrary"),
        name="ple",
    )(r, p, w_gate, w_proj, h)


def _layer(x, p, sb_fn, gdn_conv_state, gdn_rec_state, ffn_conv_state, weights, final_norm):
    (attn_norm, w_in, sb_out_norm, gdn_conv_w, gdn_a_log, gdn_dt_bias, gdn_out_norm, w_out,
     ffn_norm, w_ffn_gate, w_ffn_up, ffn_conv_w, w_ffn_down, ple_norm, w_ple_gate,
     w_ple_proj) = weights
    b, t, d = x.shape
    m = b * t
    tm = min(ROW_TILE, m)
    x2 = x.reshape(m, d)

    a = _rmsnorm(x2, attn_norm, BF16, tm)
    proj = _proj(a, w_in, tm)
    ab, abt = _gate_proj(a, w_in[:, w_in.shape[1] - LANES:], tm)
    mix_sb = sb_fn(proj)

    conv_in = proj[:, O_GDN_QKV:O_GDN_Z].reshape(b, t, GDN_CONV_CH)
    if t % GDN_C == 0:
        mix_gdn, rec = _gdn(proj, O_GDN_QKV // GDN_CONV_CH, O_GDN_Z // GDN_WIDTH, ab, abt,
                            gdn_conv_state, gdn_rec_state, gdn_conv_w, gdn_a_log, gdn_dt_bias,
                            gdn_out_norm, b, t, t)
    else:
        t_pad = -(-t // GDN_C) * GDN_C
        pad_rows = lambda arr: jnp.pad(arr.reshape(b, t, -1), ((0, 0), (0, t_pad - t), (0, 0))
                                       ).reshape(b * t_pad, -1)
        src = pad_rows(proj[:, O_GDN_QKV:])
        abt_pad = jnp.pad(abt.reshape(LANES, b, t), ((0, 0), (0, 0), (0, t_pad - t))
                          ).reshape(LANES, b * t_pad)
        mix_gdn, rec = _gdn(src, 0, GDN_CONV_CH // GDN_WIDTH, pad_rows(ab), abt_pad,
                            gdn_conv_state, gdn_rec_state, gdn_conv_w, gdn_a_log, gdn_dt_bias,
                            gdn_out_norm, b, t_pad, t)
        mix_gdn = mix_gdn.reshape(b, t_pad, GDN_WIDTH)[:, :t].reshape(m, GDN_WIDTH)
    new_gdn_conv = jnp.concatenate([gdn_conv_state, conv_in], axis=1)[:, t:]

    h = _outproj(mix_sb, mix_gdn, w_out, x2, tm)
    f = _rmsnorm(h, ffn_norm, BF16, tm)
    act, new_ffn_conv = _ffn_up(f, w_ffn_gate, w_ffn_up, ffn_conv_w, ffn_conv_state, tm, t)
    h = _ffn_down(act, w_ffn_down, h, tm)
    r = _rmsnorm(h, ple_norm, BF16, tm)
    h = _ple(r, p.reshape(m, -1), w_ple_gate, w_ple_proj, h, tm)
    y = _rmsnorm(h, final_norm, F32, tm)

    sb_k = proj[:, SB_WIDTH:2 * SB_WIDTH].reshape(b, t, SB_HEADS, HEAD_DIM)
    sb_v = proj[:, 2 * SB_WIDTH:3 * SB_WIDTH].reshape(b, t, SB_HEADS, HEAD_DIM)
    return y.reshape(b, t, d), sb_k, sb_v, new_gdn_conv, rec, new_ffn_conv


def kernel(x_prompt, x_sample, cache_sb_k, cache_sb_v, page_table, state_gdn_conv, state_gdn_rec, state_ffn_conv, p_prompt, p_sample, attn_norm, w_in, sb_logit_bias, sb_out_norm, gdn_conv_w, gdn_a_log, gdn_dt_bias, gdn_out_norm, w_out, ffn_norm, w_ffn_gate, w_ffn_up, ffn_conv_w, w_ffn_down, ple_norm, w_ple_gate, w_ple_proj, final_norm):
    assert attn_norm.shape[0] == 1, "single-layer step"
    assert x_sample.shape[1] == 1, "decode group carries one token per sequence"
    bp, tp, _ = x_prompt.shape
    bs = x_sample.shape[0]
    weights = (attn_norm[0], w_in[0], sb_out_norm[0], gdn_conv_w[0], gdn_a_log[0], gdn_dt_bias[0],
               gdn_out_norm[0], w_out[0], ffn_norm[0], w_ffn_gate[0], w_ffn_up[0], ffn_conv_w[0],
               w_ffn_down[0], ple_norm[0], w_ple_gate[0], w_ple_proj[0])
    bias = sb_logit_bias[0]
    out_norm = sb_out_norm[0]

    zeros = lambda *shape: jnp.zeros(shape, x_prompt.dtype)
    out_p = _layer(
        x_prompt, p_prompt[0],
        lambda proj: _sb_prompt(proj, bias, out_norm, bp, tp),
        zeros(bp, GDN_CONV - 1, GDN_CONV_CH), zeros(bp, GDN_HEADS, HEAD_DIM, HEAD_DIM),
        zeros(bp, FFN_CONV - 1, w_ffn_gate.shape[2]), weights, final_norm)
    out_s = _layer(
        x_sample, p_sample[0],
        lambda proj: _sb_decode(proj, cache_sb_k[0], cache_sb_v[0], page_table, bias, out_norm),
        state_gdn_conv[0], state_gdn_rec[0], state_ffn_conv[0], weights, final_norm)

    (y_p, *st_p), (y_s, *st_s) = out_p, out_s
    return (y_p, y_s, *[s[None] for s in st_p], *[s[None] for s in st_s])
```

```python
import functools

import jax
import jax.numpy as jnp
from jax import lax
from jax.experimental import pallas as pl
from jax.experimental.pallas import tpu as pltpu

F32 = jnp.float32
BF16 = jnp.bfloat16

EPS = 1e-6
HEAD_DIM = 128
SB_HEADS = 8
GDN_HEADS = 8
SB_WIDTH = SB_HEADS * HEAD_DIM
GDN_WIDTH = GDN_HEADS * HEAD_DIM
GDN_CONV = 4
GDN_CONV_CH = 3 * GDN_WIDTH
FFN_CONV = 3
SB_SCALE = HEAD_DIM ** -0.5
O_GDN_QKV = 3 * SB_WIDTH
O_GDN_Z = O_GDN_QKV + GDN_CONV_CH
PROJ_MAIN = O_GDN_Z + GDN_WIDTH

LANES = 128
SUBLANES = 8
VMEM_LIMIT = 56 * 1024 * 1024
COL_TILE = 512
ROW_TILE = 512
WIDE_TILE = 1024
SB_TQ = 512
SB_PAGES_PER_STEP = 8
GDN_C = 128
A_LANE = LANES - 2 * GDN_HEADS
B_LANE = LANES - GDN_HEADS

_NT = (((1,), (1,)), ((), ()))


def _params(*sem):
    return pltpu.CompilerParams(dimension_semantics=sem, vmem_limit_bytes=VMEM_LIMIT)


def _dot(a, b):
    return jnp.dot(a, b, preferred_element_type=F32)


def _dot_nt(a, b):
    return lax.dot_general(a, b, _NT, preferred_element_type=F32)


def _softplus(x):
    return jnp.maximum(x, 0.0) + jnp.log1p(jnp.exp(-jnp.abs(x)))


def _softplus_fast(x):
    return jnp.maximum(x, 0.0) + jnp.log(1.0 + jnp.exp(-jnp.abs(x)))


def _sigmoid(x):
    return 1.0 / (1.0 + jnp.exp(-x))


def _split2(x):
    hi = x.astype(BF16)
    lo = (x - hi.astype(F32)).astype(BF16)
    return hi, lo


def _split3(x):
    h1 = x.astype(BF16)
    r1 = x - h1.astype(F32)
    h2 = r1.astype(BF16)
    h3 = (r1 - h2.astype(F32)).astype(BF16)
    return h1, h2, h3


def _rmsnorm_kernel(x_ref, w_ref, o_ref):
    x = x_ref[...]
    y = x * lax.rsqrt(jnp.mean(x * x, axis=-1, keepdims=True) + EPS)
    o_ref[...] = (y * w_ref[...]).astype(o_ref.dtype)


def _rmsnorm(x, w, out_dtype, tm):
    m, d = x.shape
    return pl.pallas_call(
        _rmsnorm_kernel,
        grid=(m // tm,),
        in_specs=[pl.BlockSpec((tm, d), lambda i: (i, 0)),
                  pl.BlockSpec((1, d), lambda i: (0, 0))],
        out_specs=pl.BlockSpec((tm, d), lambda i: (i, 0)),
        out_shape=jax.ShapeDtypeStruct((m, d), out_dtype),
        compiler_params=_params("parallel"),
        name="rmsnorm",
    )(x, w.reshape(1, d))


def _proj_kernel(a_ref, w_ref, o_ref, wbf_ref):
    @pl.when(pl.program_id(1) == 0)
    def _():
        wbf_ref[...] = w_ref[...].astype(BF16)

    o_ref[...] = _dot(a_ref[...], wbf_ref[...])


def _many_rows(m):
    return m % WIDE_TILE == 0


def _weight_spec(shape, index_map, m):
    return pl.BlockSpec(shape, index_map, pipeline_mode=pl.Buffered(1 if _many_rows(m) else 2))


def _proj(a, w_in):
    m, d = a.shape
    tm, tn = (WIDE_TILE, WIDE_TILE) if _many_rows(m) else (m, COL_TILE)
    return pl.pallas_call(
        _proj_kernel,
        grid=(PROJ_MAIN // tn, m // tm),
        in_specs=[pl.BlockSpec((tm, d), lambda j, i: (i, 0)),
                  _weight_spec((d, tn), lambda j, i: (0, j), m)],
        out_specs=pl.BlockSpec((tm, tn), lambda j, i: (i, j)),
        out_shape=jax.ShapeDtypeStruct((m, PROJ_MAIN), F32),
        scratch_shapes=[pltpu.VMEM((d, tn), BF16)],
        compiler_params=_params("arbitrary", "arbitrary"),
        name="proj",
    )(a, w_in)


def _gate_proj_kernel(a_ref, w_ref, wt_ref, ab_ref, abt_ref):
    a = a_ref[...]
    ab_ref[...] = _dot(a, w_ref[...])
    abt_ref[...] = _dot_nt(wt_ref[...], a)


def _gate_proj(a, w_tail, tm):
    m, d = a.shape
    w = w_tail.astype(BF16)
    return pl.pallas_call(
        _gate_proj_kernel,
        grid=(m // tm,),
        in_specs=[pl.BlockSpec((tm, d), lambda i: (i, 0)),
                  pl.BlockSpec((d, LANES), lambda i: (0, 0)),
                  pl.BlockSpec((LANES, d), lambda i: (0, 0))],
        out_specs=[pl.BlockSpec((tm, LANES), lambda i: (i, 0)),
                   pl.BlockSpec((LANES, tm), lambda i: (0, i))],
        out_shape=[jax.ShapeDtypeStruct((m, LANES), F32),
                   jax.ShapeDtypeStruct((LANES, m), F32)],
        compiler_params=_params("parallel"),
        name="gate_proj",
    )(a, w, w.T)


def _cumsum_weights():
    j = lax.broadcasted_iota(jnp.int32, (LANES, LANES), 0)
    s = lax.broadcasted_iota(jnp.int32, (LANES, LANES), 1)
    incl = (j >= s).astype(BF16)
    half = jnp.concatenate([incl, jnp.ones((LANES, LANES), BF16)], axis=1)
    return jnp.concatenate([half, half], axis=0)


def _sb_chunk(z, vc, w2, r, diag):
    n = vc.shape[0] // LANES
    blocks = []
    for c in range(n):
        row0 = c * LANES if diag else 0
        zc = z[row0:, c * LANES:(c + 1) * LANES]
        sp = _softplus_fast(zc)
        valid = None
        if diag:
            valid = (lax.broadcasted_iota(jnp.int32, zc.shape, 1)
                     < lax.broadcasted_iota(jnp.int32, zc.shape, 0))
            sp = jnp.where(valid, sp, 0.0)
        hi, lo = _split2(sp)
        blocks.append((row0, zc, valid, jnp.concatenate([hi, lo], axis=1)))
    res = _dot(jnp.concatenate([blk[3] for blk in blocks], axis=0), w2)
    ends = []
    for blk in blocks:
        ends.append((ends[-1] if ends else 0) + blk[1].shape[0])
    a_parts = [None] * n
    for c in reversed(range(n)):
        row0, zc, valid, _ = blocks[c]
        rc = res[ends[c] - zc.shape[0]:ends[c]]
        a = jnp.exp(zc - rc[:, :LANES] - r[row0:])
        if diag:
            a = jnp.where(valid, a, 0.0)
        a = a.astype(BF16)
        r_new = r[row0:] + rc[:, LANES:]
        if row0:
            a = jnp.concatenate([jnp.zeros((row0, LANES), BF16), a], axis=0)
            r_new = jnp.concatenate([r[:row0], r_new], axis=0)
        a_parts[c] = a
        r = r_new
    return _dot(jnp.concatenate(a_parts, axis=1), vc), r


def _sb_prompt_kernel(c_ref, q_ref, k_ref, v_ref, w2_ref, nw_ref, o_ref, kbf, vbf, acc, run, zbuf):
    h = pl.program_id(1)
    i = pl.program_id(2)

    @pl.when(i == 0)
    def _():
        kbf[...] = k_ref[...].astype(BF16)
        vbf[...] = v_ref[...].astype(BF16)

    z_bias = c_ref[h]
    w2 = w2_ref[...]
    q = (q_ref[...] * SB_SCALE).astype(BF16)

    def rows_of(chunk):
        return pl.ds(pl.multiple_of(jnp.maximum(chunk, 0) * SB_TQ, SB_TQ), SB_TQ)

    def logits(chunk):
        return _dot_nt(q, kbf[rows_of(chunk), :]) + z_bias

    zbuf[0] = logits(i - 1)
    d, r = _sb_chunk(logits(i), vbf[rows_of(i), :], w2, jnp.zeros(run.shape, F32), True)
    acc[...] = d
    run[...] = r

    def step(chunk, slot, more):
        if more:
            zbuf[1 - slot] = logits(chunk - 1)
        d, r = _sb_chunk(zbuf[slot], vbf[rows_of(chunk), :], w2, run[...], False)
        acc[...] += d
        run[...] = r

    def body(j, carry):
        step(i - 1 - 2 * j, 0, True)
        step(i - 2 - 2 * j, 1, True)
        return carry

    lax.fori_loop(0, i // 2, body, 0)

    @pl.when(i % 2 == 1)
    def _():
        step(0, 0, False)

    o = acc[...]
    y = o * lax.rsqrt(jnp.mean(o * o, axis=-1, keepdims=True) + EPS) * nw_ref[...]
    o_ref[...] = y.astype(o_ref.dtype)


def _sb_prompt(proj, logit_bias, out_norm, b, t):
    nq = t // SB_TQ
    blk = lambda rows, fn: pl.BlockSpec((rows, HEAD_DIM), fn)
    return pl.pallas_call(
        _sb_prompt_kernel,
        grid=(b, SB_HEADS, nq),
        in_specs=[pl.BlockSpec(memory_space=pltpu.SMEM),
                  blk(SB_TQ, lambda bi, h, i: (bi * nq + i, h)),
                  blk(t, lambda bi, h, i: (bi, SB_HEADS + h)),
                  blk(t, lambda bi, h, i: (bi, 2 * SB_HEADS + h)),
                  pl.BlockSpec((2 * LANES, 2 * LANES), lambda bi, h, i: (0, 0)),
                  pl.BlockSpec((1, HEAD_DIM), lambda bi, h, i: (0, 0))],
        out_specs=blk(SB_TQ, lambda bi, h, i: (bi * nq + i, h)),
        out_shape=jax.ShapeDtypeStruct((b * t, SB_WIDTH), BF16),
        scratch_shapes=[pltpu.VMEM((t, HEAD_DIM), BF16), pltpu.VMEM((t, HEAD_DIM), BF16),
                        pltpu.VMEM((SB_TQ, HEAD_DIM), F32), pltpu.VMEM((SB_TQ, LANES), F32),
                        pltpu.VMEM((2, SB_TQ, SB_TQ), F32)],
        compiler_params=_params("arbitrary", "arbitrary", "arbitrary"),
        name="sb_prompt",
    )(logit_bias, proj, proj, proj, _cumsum_weights(), out_norm.reshape(1, HEAD_DIM))


def _sb_decode_kernel(pt_ref, q_ref, kn_ref, vn_ref, *rest, past_len):
    k_refs = rest[:SB_PAGES_PER_STEP]
    v_refs = rest[SB_PAGES_PER_STEP:2 * SB_PAGES_PER_STEP]
    c_ref, nw_ref, o_ref, acc, run = rest[2 * SB_PAGES_PER_STEP:]
    p = pl.program_id(1)
    page = k_refs[0].shape[1]
    flat = SB_PAGES_PER_STEP * page * SB_HEADS
    lane = lax.broadcasted_iota(jnp.int32, (SB_HEADS, flat), 1)
    own = (lane % SB_HEADS) == lax.broadcasted_iota(jnp.int32, (SB_HEADS, flat), 0)
    z_bias = c_ref[...][:, :1]
    qb = (q_ref[0] * SB_SCALE).astype(BF16)

    @pl.when(p == 0)
    def _():
        kn = kn_ref[0].astype(BF16).astype(F32)
        z = jnp.sum(qb.astype(F32) * kn, axis=-1, keepdims=True) + z_bias
        valid = jnp.full(z.shape, past_len, jnp.int32) < past_len
        sp = jnp.where(valid, _softplus(z), 0.0)
        a = jnp.where(valid, jnp.exp(z - sp), 0.0)
        acc[...] = a.astype(BF16).astype(F32) * vn_ref[0].astype(BF16).astype(F32)
        run[...] = jnp.broadcast_to(sp, run.shape)

    flatten = lambda ref: ref[0].reshape(page * SB_HEADS, HEAD_DIM).astype(BF16)
    kp = jnp.concatenate([flatten(ref) for ref in reversed(k_refs)], axis=0)
    vp = jnp.concatenate([flatten(ref) for ref in reversed(v_refs)], axis=0)
    z = _dot_nt(qb, kp) + z_bias
    sp = jnp.where(own, _softplus(z), 0.0)
    s = sp
    step = SB_HEADS
    while step < flat:
        s = s + jnp.where(lane + step < flat, pltpu.roll(s, flat - step, axis=1), 0.0)
        step *= 2
    r = run[...]
    a = jnp.where(own, jnp.exp(z - s - r[:, :1]), 0.0)
    run[...] = r + jnp.sum(sp, axis=1, keepdims=True)
    acc[...] += _dot(a.astype(BF16), vp)

    @pl.when(p == pl.num_programs(1) - 1)
    def _():
        o = acc[...]
        y = o * lax.rsqrt(jnp.mean(o * o, axis=-1, keepdims=True) + EPS) * nw_ref[...]
        o_ref[0] = y.astype(o_ref.dtype)


def _sb_decode(proj, cache_k, cache_v, page_table, logit_bias, out_norm):
    b = proj.shape[0]
    _, page, _, _ = cache_k.shape
    n_pages = page_table.shape[1]
    pps = SB_PAGES_PER_STEP
    assert n_pages % pps == 0
    proj3 = proj.reshape(b, PROJ_MAIN // HEAD_DIM, HEAD_DIM)
    row = lambda blk: pl.BlockSpec((1, SB_HEADS, HEAD_DIM), lambda bi, p, pt: (bi, blk, 0))
    pg = lambda s: pl.BlockSpec(
        (1, page, SB_HEADS, HEAD_DIM),
        lambda bi, p, pt: (pt[bi * n_pages + n_pages - 1 - (p * pps + s)], 0, 0, 0))
    pages = [pg(s) for s in range(pps)]
    const = lambda shape: pl.BlockSpec(shape, lambda bi, p, pt: (0,) * len(shape))
    grid_spec = pltpu.PrefetchScalarGridSpec(
        num_scalar_prefetch=1,
        grid=(b, n_pages // pps),
        in_specs=[row(0), row(1), row(2), *pages, *pages,
                  const((SB_HEADS, LANES)), const((1, HEAD_DIM))],
        out_specs=pl.BlockSpec((1, SB_HEADS, HEAD_DIM), lambda bi, p, pt: (bi, 0, 0)),
        scratch_shapes=[pltpu.VMEM((SB_HEADS, HEAD_DIM), F32),
                        pltpu.VMEM((SB_HEADS, LANES), F32)],
    )
    out = pl.pallas_call(
        functools.partial(_sb_decode_kernel, past_len=n_pages * page),
        grid_spec=grid_spec,
        out_shape=jax.ShapeDtypeStruct((b, SB_HEADS, HEAD_DIM), BF16),
        compiler_params=_params("arbitrary", "arbitrary"),
        name="sb_decode",
    )(page_table.reshape(-1), proj3, proj3, proj3, *([cache_k] * pps), *([cache_v] * pps),
      jnp.broadcast_to(logit_bias[:, None], (SB_HEADS, LANES)), out_norm.reshape(1, HEAD_DIM))
    return out.reshape(b, SB_WIDTH)


def _shift_rows(x, prev, k):
    xr = pltpu.roll(x, k, axis=0)
    row = lax.broadcasted_iota(jnp.int32, prev.shape, 0)
    top = jnp.where(row < k, pltpu.roll(prev, k, axis=0), xr[:SUBLANES])
    return jnp.concatenate([top, xr[SUBLANES:]], axis=0)


def _gdn_kernel(cin_ref, z_ref, ab_ref, abt_ref, cst_ref, s0_ref, cw_ref, prow_ref, pcol_ref,
                nw_ref, o_ref, sfin_ref, s_scr, hist_scr, *, t_valid, t_total):
    c_len = GDN_C
    t = pl.program_id(1)

    @pl.when(t == 0)
    def _():
        s_scr[...] = s0_ref[0]
        hist_scr[...] = cst_ref[0]

    x = cin_ref[...]
    hist = hist_scr[...]
    cw = cw_ref[...]
    y = x * cw[3:4]
    for kk in range(1, GDN_CONV):
        y = y + _shift_rows(x, hist, kk) * cw[GDN_CONV - 1 - kk:GDN_CONV - kk]
    hist_scr[...] = x[c_len - SUBLANES:]
    conv = y * _sigmoid(y)

    ri = lax.broadcasted_iota(jnp.int32, (c_len, c_len), 0)
    ci = lax.broadcasted_iota(jnp.int32, (c_len, c_len), 1)
    tri = ri >= ci
    strict = ri > ci
    eye = (ri == ci).astype(F32)
    eye_bf = eye.astype(BF16)
    ltri = tri.astype(BF16)
    utri = (ri <= ci).astype(BF16)

    ab = ab_ref[...]
    g_col = -jnp.exp(prow_ref[0:1]) * _softplus(ab + prow_ref[1:2])
    beta_col = _sigmoid(ab)
    abt = abt_ref[0]
    g_row = (-jnp.exp(pcol_ref[0:GDN_HEADS]) *
             _softplus(abt[0:GDN_HEADS] + pcol_ref[GDN_HEADS:2 * GDN_HEADS]))
    if t_valid < t_total:
        pos_c = t * c_len + lax.broadcasted_iota(jnp.int32, (c_len, LANES), 0)
        g_col = jnp.where(pos_c < t_valid, g_col, 0.0)
        beta_col = jnp.where(pos_c < t_valid, beta_col, 0.0)
        pos_r = t * c_len + lax.broadcasted_iota(jnp.int32, (GDN_HEADS, c_len), 1)
        g_row = jnp.where(pos_r < t_valid, g_row, 0.0)
    gc_col = sum(_dot(ltri, part) for part in _split3(g_col))
    gc_row = sum(_dot(part, utri) for part in _split3(g_row))

    levels = []
    m = 1
    while m < c_len:
        lm = m.bit_length() - 1
        same = ((ri ^ ci) >> (lm + 1)) == 0
        lower = ((ri >> lm) & 1) > ((ci >> lm) & 1)
        levels.append(jnp.where(same, jnp.where(lower, 1.0, 0.0), 0.0))
        m *= 2

    heads = range(GDN_HEADS)
    head_cols = lambda base, h: slice(base + h * HEAD_DIM, base + (h + 1) * HEAD_DIM)
    qn, kn, kn_bf, kb, vb, decay, egc, gcc = [], [], [], [], [], [], [], []
    for h in heads:
        q = conv[:, head_cols(0, h)]
        k = conv[:, head_cols(GDN_WIDTH, h)]
        v = conv[:, head_cols(2 * GDN_WIDTH, h)]
        qn.append(q * lax.rsqrt(jnp.sum(q * q, axis=-1, keepdims=True) + 1e-6) * (HEAD_DIM ** -0.5))
        kn.append(k * lax.rsqrt(jnp.sum(k * k, axis=-1, keepdims=True) + 1e-6))
        kn_bf.append(kn[h].astype(BF16))
        gcc.append(gc_col[:, A_LANE + h:A_LANE + h + 1])
        beta = beta_col[:, B_LANE + h:B_LANE + h + 1]
        gcr = gc_row[h:h + 1, :]
        kb.append(kn[h] * beta)
        vb.append(v * beta)
        decay.append(jnp.where(tri, jnp.exp(jnp.where(tri, gcc[h] - gcr, 0.0)), 0.0))
        egc.append(jnp.exp(gcc[h]))
    low = [jnp.where(strict, _dot_nt(kb[h].astype(BF16), kn_bf[h]) * decay[h], 0.0) for h in heads]
    qk = [(_dot_nt(qn[h].astype(BF16), kn_bf[h]) * decay[h]).astype(BF16) for h in heads]

    xinv = [eye - low[h] * levels[0] for h in heads]
    for lvl in levels[1:]:
        xb = [xinv[h].astype(BF16) for h in heads]
        y1 = [_dot(xb[h], (low[h] * lvl).astype(BF16)).astype(BF16) for h in heads]
        xinv = [xinv[h] - _dot(y1[h], xb[h]) for h in heads]

    sol = []
    for h in heads:
        rhs = jnp.concatenate([vb[h], kb[h] * egc[h]], axis=1)
        sol.append(rhs + _dot((xinv[h] - eye).astype(BF16), rhs.astype(BF16)))
    s_old = [s_scr[h] for h in heads]
    ws_qs = [_dot(jnp.concatenate([sol[h][:, HEAD_DIM:], qn[h] * egc[h]], axis=0).astype(BF16),
                  s_old[h].astype(BF16)) for h in heads]
    v_new = [(sol[h][:, :HEAD_DIM] - ws_qs[h][:c_len]).astype(BF16) for h in heads]
    out = [ws_qs[h][c_len:] + _dot(qk[h], v_new[h]) for h in heads]
    nw = nw_ref[...]
    for h in heads:
        g_last = gcc[h][c_len - 1:c_len, :]
        kd = (kn[h] * jnp.exp(g_last - gcc[h])).astype(BF16)
        kd_t = _dot_nt(eye_bf, kd).astype(BF16)
        s_scr[h] = s_old[h] * jnp.exp(g_last) + _dot(kd_t, v_new[h])
    for h in heads:
        zg = z_ref[:, head_cols(0, h)]
        o = out[h]
        on = o * lax.rsqrt(jnp.mean(o * o, axis=-1, keepdims=True) + EPS) * nw
        o_ref[:, head_cols(0, h)] = (on * (zg * _sigmoid(zg))).astype(o_ref.dtype)

    @pl.when(t == pl.num_programs(1) - 1)
    def _():
        sfin_ref[0] = s_scr[...]


def _gdn(src, cin_blk, z_blk, ab, abt, conv_state, rec_state, conv_w, a_log, dt_bias, out_norm,
         b, t_pad, t_valid):
    c_len = GDN_C
    nt = t_pad // c_len
    cst = jnp.pad(conv_state, ((0, 0), (SUBLANES - (GDN_CONV - 1), 0), (0, 0)))
    cw = jnp.pad(conv_w, ((0, SUBLANES - GDN_CONV), (0, 0)))
    prow = jnp.zeros((SUBLANES, LANES), F32)
    prow = prow.at[0, A_LANE:B_LANE].set(a_log).at[1, A_LANE:B_LANE].set(dt_bias)
    pcol = jnp.broadcast_to(jnp.concatenate([a_log, dt_bias])[:, None], (2 * GDN_HEADS, LANES))
    abt3 = abt[A_LANE:].reshape(2 * GDN_HEADS, b * nt, c_len).transpose(1, 0, 2)
    full = lambda shape: pl.BlockSpec(shape, lambda bi, ti: (0,) * len(shape))
    return pl.pallas_call(
        functools.partial(_gdn_kernel, t_valid=t_valid, t_total=t_pad),
        grid=(b, nt),
        in_specs=[pl.BlockSpec((c_len, GDN_CONV_CH), lambda bi, ti: (bi * nt + ti, cin_blk)),
                  pl.BlockSpec((c_len, GDN_WIDTH), lambda bi, ti: (bi * nt + ti, z_blk)),
                  pl.BlockSpec((c_len, LANES), lambda bi, ti: (bi * nt + ti, 0)),
                  pl.BlockSpec((1, 2 * GDN_HEADS, c_len), lambda bi, ti: (bi * nt + ti, 0, 0)),
                  pl.BlockSpec((1, SUBLANES, GDN_CONV_CH), lambda bi, ti: (bi, 0, 0)),
                  pl.BlockSpec((1, GDN_HEADS, HEAD_DIM, HEAD_DIM), lambda bi, ti: (bi, 0, 0, 0)),
                  full((SUBLANES, GDN_CONV_CH)), full((SUBLANES, LANES)),
                  full((2 * GDN_HEADS, LANES)), full((1, HEAD_DIM))],
        out_specs=[pl.BlockSpec((c_len, GDN_WIDTH), lambda bi, ti: (bi * nt + ti, 0)),
                   pl.BlockSpec((1, GDN_HEADS, HEAD_DIM, HEAD_DIM), lambda bi, ti: (bi, 0, 0, 0))],
        out_shape=[jax.ShapeDtypeStruct((b * t_pad, GDN_WIDTH), BF16),
                   jax.ShapeDtypeStruct((b, GDN_HEADS, HEAD_DIM, HEAD_DIM), F32)],
        scratch_shapes=[pltpu.VMEM((GDN_HEADS, HEAD_DIM, HEAD_DIM), F32),
                        pltpu.VMEM((SUBLANES, GDN_CONV_CH), F32)],
        compiler_params=_params("arbitrary", "arbitrary"),
        name="gdn",
    )(src, src, ab, abt3, cst, rec_state, cw, prow, pcol, out_norm.reshape(1, HEAD_DIM))


def _rms(x, w):
    return x * lax.rsqrt(jnp.mean(x * x, axis=-1, keepdims=True) + EPS) * w


def _outproj_kernel(ms_ref, mg_ref, w0_ref, w1_ref, x_ref, nw_ref, h_ref, f_ref, w0_bf, w1_bf):
    @pl.when(pl.program_id(0) == 0)
    def _():
        w0_bf[...] = w0_ref[...].astype(BF16)
        w1_bf[...] = w1_ref[...].astype(BF16)

    h = x_ref[...] + _dot(ms_ref[...], w0_bf[...]) + _dot(mg_ref[...], w1_bf[...])
    h_ref[...] = h
    f_ref[...] = _rms(h, nw_ref[...]).astype(f_ref.dtype)


def _outproj(mix_sb, mix_gdn, w_out, x, ffn_norm):
    m, d = x.shape
    tm = min(ROW_TILE, m)
    return pl.pallas_call(
        _outproj_kernel,
        grid=(m // tm,),
        in_specs=[pl.BlockSpec((tm, SB_WIDTH), lambda i: (i, 0)),
                  pl.BlockSpec((tm, GDN_WIDTH), lambda i: (i, 0)),
                  pl.BlockSpec((SB_WIDTH, d), lambda i: (0, 0), pipeline_mode=pl.Buffered(1)),
                  pl.BlockSpec((GDN_WIDTH, d), lambda i: (1, 0), pipeline_mode=pl.Buffered(1)),
                  pl.BlockSpec((tm, d), lambda i: (i, 0)),
                  pl.BlockSpec((1, d), lambda i: (0, 0))],
        out_specs=[pl.BlockSpec((tm, d), lambda i: (i, 0)), pl.BlockSpec((tm, d), lambda i: (i, 0))],
        out_shape=[jax.ShapeDtypeStruct((m, d), F32), jax.ShapeDtypeStruct((m, d), BF16)],
        scratch_shapes=[pltpu.VMEM((SB_WIDTH, d), BF16), pltpu.VMEM((GDN_WIDTH, d), BF16)],
        compiler_params=_params("arbitrary"),
        name="outproj",
    )(mix_sb, mix_gdn, w_out, w_out, x, ffn_norm.reshape(1, d))


def _ffn_up_kernel(f_ref, wg_ref, wu_ref, cw_ref, hist_ref, act_ref, st_ref, wg_bf, wu_bf, carry,
                   *, tiles_per_seq, single_token):
    i = pl.program_id(1)

    @pl.when(i == 0)
    def _():
        wg_bf[...] = wg_ref[...].astype(BF16)
        wu_bf[...] = wu_ref[...].astype(BF16)

    f = f_ref[...]
    g = _dot(f, wg_bf[...])
    u = _dot(f, wu_bf[...])
    cw = cw_ref[...]
    if single_token:
        h0 = hist_ref[0]
        h1 = hist_ref[1]
        gate = h0 * cw[0:1] + h1 * cw[1:2] + g * cw[2:3]
        st_ref[0] = h1
        st_ref[1] = g
    else:
        @pl.when(i % tiles_per_seq == 0)
        def _():
            carry[...] = hist_ref[0]

        prev = carry[...]
        gate = (_shift_rows(g, prev, 2) * cw[0:1] + _shift_rows(g, prev, 1) * cw[1:2]
                + g * cw[2:3])
        last = g[g.shape[0] - SUBLANES:]
        carry[...] = last
        st_ref[0] = last
    act_ref[...] = (gate * _sigmoid(gate) * u).astype(act_ref.dtype)


def _ffn_up(f, w_gate, w_up, conv_w, state, seq_len):
    m, d = f.shape
    dff = w_gate.shape[1]
    tm = WIDE_TILE if _many_rows(m) else m
    tn = COL_TILE
    nj = pl.cdiv(dff, tn)
    cw = jnp.pad(conv_w, ((0, SUBLANES - FFN_CONV), (0, 0)))
    single = seq_len == 1
    nb = m // seq_len
    if single:
        assert tm == m
        hist = state.transpose(1, 0, 2)
        hist_spec = pl.BlockSpec((FFN_CONV - 1, m, tn), lambda j, i: (0, 0, j))
        st_spec = pl.BlockSpec((FFN_CONV - 1, m, tn), lambda j, i: (0, 0, j))
        st_shape = jax.ShapeDtypeStruct((FFN_CONV - 1, m, dff), F32)
        tiles_per_seq = 1
    else:
        tiles_per_seq = seq_len // tm
        hist = jnp.pad(state, ((0, 0), (SUBLANES - (FFN_CONV - 1), 0), (0, 0)))
        hist_spec = pl.BlockSpec((1, SUBLANES, tn), lambda j, i: (i // tiles_per_seq, 0, j))
        st_spec = pl.BlockSpec((1, SUBLANES, tn), lambda j, i: (i // tiles_per_seq, 0, j))
        st_shape = jax.ShapeDtypeStruct((nb, SUBLANES, dff), F32)
    act, st = pl.pallas_call(
        functools.partial(_ffn_up_kernel, tiles_per_seq=tiles_per_seq, single_token=single),
        grid=(nj, m // tm),
        in_specs=[pl.BlockSpec((tm, d), lambda j, i: (i, 0)),
                  pl.BlockSpec((d, tn), lambda j, i: (0, j)),
                  pl.BlockSpec((d, tn), lambda j, i: (0, j)),
                  pl.BlockSpec((SUBLANES, tn), lambda j, i: (0, j)),
                  hist_spec],
        out_specs=[pl.BlockSpec((tm, tn), lambda j, i: (i, j)), st_spec],
        out_shape=[jax.ShapeDtypeStruct((m, dff), BF16), st_shape],
        scratch_shapes=[pltpu.VMEM((d, tn), BF16), pltpu.VMEM((d, tn), BF16),
                        pltpu.VMEM((SUBLANES, tn), F32)],
        compiler_params=_params("arbitrary", "arbitrary"),
        name="ffn_up",
    )(f, w_gate, w_up, cw, hist)
    if single:
        return act, st.transpose(1, 0, 2)
    return act, st[:, SUBLANES - (FFN_CONV - 1):]


def _ffn_down_kernel(act_ref, w_ref, h_ref, o_ref, w_bf):
    @pl.when(pl.program_id(1) == 0)
    def _():
        w_bf[...] = w_ref[...].astype(BF16)

    o_ref[...] = h_ref[...] + _dot(act_ref[...], w_bf[...])


def _ffn_down(act, w_down, h):
    m, dff = act.shape
    d = h.shape[1]
    tm, tn = (ROW_TILE, WIDE_TILE) if _many_rows(m) else (m, COL_TILE)
    return pl.pallas_call(
        _ffn_down_kernel,
        grid=(d // tn, m // tm),
        in_specs=[pl.BlockSpec((tm, dff), lambda j, i: (i, 0)),
                  _weight_spec((dff, tn), lambda j, i: (0, j), m),
                  pl.BlockSpec((tm, tn), lambda j, i: (i, j))],
        out_specs=pl.BlockSpec((tm, tn), lambda j, i: (i, j)),
        out_shape=jax.ShapeDtypeStruct((m, d), F32),
        scratch_shapes=[pltpu.VMEM((dff, tn), BF16)],
        compiler_params=_params("arbitrary", "arbitrary"),
        name="ffn_down",
    )(act, w_down, h)


def _ple_kernel(r_ref, p_ref, wg_ref, wp_ref, h_ref, nw_ref, o_ref, wg_bf, wp_bf):
    @pl.when(pl.program_id(0) == 0)
    def _():
        wg_bf[...] = wg_ref[...].astype(BF16)
        wp_bf[...] = wp_ref[...].astype(BF16)

    gate = _sigmoid(_dot(r_ref[...], wg_bf[...]))
    emb = _dot(p_ref[...].astype(BF16), wp_bf[...])
    o_ref[...] = _rms(h_ref[...] + emb * gate, nw_ref[...])


def _ple(r, p, w_gate, w_proj, h, final_norm):
    m, d = h.shape
    pd = p.shape[1]
    tm = min(ROW_TILE, m)
    return pl.pallas_call(
        _ple_kernel,
        grid=(m // tm,),
        in_specs=[pl.BlockSpec((tm, d), lambda i: (i, 0)),
                  pl.BlockSpec((tm, pd), lambda i: (i, 0)),
                  pl.BlockSpec((d, d), lambda i: (0, 0), pipeline_mode=pl.Buffered(1)),
                  pl.BlockSpec((pd, d), lambda i: (0, 0), pipeline_mode=pl.Buffered(1)),
                  pl.BlockSpec((tm, d), lambda i: (i, 0)),
                  pl.BlockSpec((1, d), lambda i: (0, 0))],
        out_specs=pl.BlockSpec((tm, d), lambda i: (i, 0)),
        out_shape=jax.ShapeDtypeStruct((m, d), F32),
        scratch_shapes=[pltpu.VMEM((d, d), BF16), pltpu.VMEM((pd, d), BF16)],
        compiler_params=_params("arbitrary"),
        name="ple",
    )(r, p, w_gate, w_proj, h, final_norm.reshape(1, d))


def _layer(x, p, sb_fn, gdn_conv_state, gdn_rec_state, ffn_conv_state, weights, final_norm):
    (attn_norm, w_in, sb_out_norm, gdn_conv_w, gdn_a_log, gdn_dt_bias, gdn_out_norm, w_out,
     ffn_norm, w_ffn_gate, w_ffn_up, ffn_conv_w, w_ffn_down, ple_norm, w_ple_gate,
     w_ple_proj) = weights
    b, t, d = x.shape
    m = b * t
    tm = min(ROW_TILE, m)
    x2 = x.reshape(m, d)

    a = _rmsnorm(x2, attn_norm, BF16, tm)
    proj = _proj(a, w_in)
    ab, abt = _gate_proj(a, w_in[:, w_in.shape[1] - LANES:], tm)
    mix_sb = sb_fn(proj)

    conv_in = proj[:, O_GDN_QKV:O_GDN_Z].reshape(b, t, GDN_CONV_CH)
    if t % GDN_C == 0:
        mix_gdn, rec = _gdn(proj, O_GDN_QKV // GDN_CONV_CH, O_GDN_Z // GDN_WIDTH, ab, abt,
                            gdn_conv_state, gdn_rec_state, gdn_conv_w, gdn_a_log, gdn_dt_bias,
                            gdn_out_norm, b, t, t)
    else:
        t_pad = -(-t // GDN_C) * GDN_C
        pad_rows = lambda arr: jnp.pad(arr.reshape(b, t, -1), ((0, 0), (0, t_pad - t), (0, 0))
                                       ).reshape(b * t_pad, -1)
        src = pad_rows(proj[:, O_GDN_QKV:])
        abt_pad = jnp.pad(abt.reshape(LANES, b, t), ((0, 0), (0, 0), (0, t_pad - t))
                          ).reshape(LANES, b * t_pad)
        mix_gdn, rec = _gdn(src, 0, GDN_CONV_CH // GDN_WIDTH, pad_rows(ab), abt_pad,
                            gdn_conv_state, gdn_rec_state, gdn_conv_w, gdn_a_log, gdn_dt_bias,
                            gdn_out_norm, b, t_pad, t)
        mix_gdn = mix_gdn.reshape(b, t_pad, GDN_WIDTH)[:, :t].reshape(m, GDN_WIDTH)
    new_gdn_conv = jnp.concatenate([gdn_conv_state, conv_in], axis=1)[:, t:]

    h, f = _outproj(mix_sb, mix_gdn, w_out, x2, ffn_norm)
    act, new_ffn_conv = _ffn_up(f, w_ffn_gate, w_ffn_up, ffn_conv_w, ffn_conv_state, t)
    h = _ffn_down(act, w_ffn_down, h)
    r = _rmsnorm(h, ple_norm, BF16, tm)
    y = _ple(r, p.reshape(m, -1), w_ple_gate, w_ple_proj, h, final_norm)

    sb_k = proj[:, SB_WIDTH:2 * SB_WIDTH].reshape(b, t, SB_HEADS, HEAD_DIM)
    sb_v = proj[:, 2 * SB_WIDTH:3 * SB_WIDTH].reshape(b, t, SB_HEADS, HEAD_DIM)
    return y.reshape(b, t, d), sb_k, sb_v, new_gdn_conv, rec, new_ffn_conv


def kernel(x_prompt, x_sample, cache_sb_k, cache_sb_v, page_table, state_gdn_conv, state_gdn_rec, state_ffn_conv, p_prompt, p_sample, attn_norm, w_in, sb_logit_bias, sb_out_norm, gdn_conv_w, gdn_a_log, gdn_dt_bias, gdn_out_norm, w_out, ffn_norm, w_ffn_gate, w_ffn_up, ffn_conv_w, w_ffn_down, ple_norm, w_ple_gate, w_ple_proj, final_norm):
    assert attn_norm.shape[0] == 1, "single-layer step"
    assert x_sample.shape[1] == 1, "decode group carries one token per sequence"
    bp, tp, _ = x_prompt.shape
    bs = x_sample.shape[0]
    weights = (attn_norm[0], w_in[0], sb_out_norm[0], gdn_conv_w[0], gdn_a_log[0], gdn_dt_bias[0],
               gdn_out_norm[0], w_out[0], ffn_norm[0], w_ffn_gate[0], w_ffn_up[0], ffn_conv_w[0],
               w_ffn_down[0], ple_norm[0], w_ple_gate[0], w_ple_proj[0])
    bias = sb_logit_bias[0]
    out_norm = sb_out_norm[0]

    zeros = lambda *shape: jnp.zeros(shape, x_prompt.dtype)
    out_p = _layer(
        x_prompt, p_prompt[0],
        lambda proj: _sb_prompt(proj, bias, out_norm, bp, tp),
        zeros(bp, GDN_CONV - 1, GDN_CONV_CH), zeros(bp, GDN_HEADS, HEAD_DIM, HEAD_DIM),
        zeros(bp, FFN_CONV - 1, w_ffn_gate.shape[2]), weights, final_norm)
    out_s = _layer(
        x_sample, p_sample[0],
        lambda proj: _sb_decode(proj, cache_sb_k[0], cache_sb_v[0], page_table, bias, out_norm),
        state_gdn_conv[0], state_gdn_rec[0], state_ffn_conv[0], weights, final_norm)

    (y_p, *st_p), (y_s, *st_s) = out_p, out_s
    return (y_p, y_s, *[s[None] for s in st_p], *[s[None] for s in st_s])
```

```python
import functools

import jax
import jax.numpy as jnp
from jax import lax
from jax.experimental import pallas as pl
from jax.experimental.pallas import tpu as pltpu

F32 = jnp.float32
BF16 = jnp.bfloat16

EPS = 1e-6
HEAD_DIM = 128
SB_HEADS = 8
GDN_HEADS = 8
SB_WIDTH = SB_HEADS * HEAD_DIM
GDN_WIDTH = GDN_HEADS * HEAD_DIM
GDN_CONV = 4
GDN_CONV_CH = 3 * GDN_WIDTH
FFN_CONV = 3
SB_SCALE = HEAD_DIM ** -0.5
LOG2_E = 1.4426950408889634
O_GDN_QKV = 3 * SB_WIDTH
O_GDN_Z = O_GDN_QKV + GDN_CONV_CH
PROJ_MAIN = O_GDN_Z + GDN_WIDTH

LANES = 128
SUBLANES = 8
VMEM_LIMIT = 56 * 1024 * 1024
COL_TILE = 512
ROW_TILE = 512
WIDE_TILE = 1024
SUB_ROWS = 256
SB_TQ = 512
SB_PAGES_PER_STEP = 8
GDN_C = 128
A_LANE = LANES - 2 * GDN_HEADS
B_LANE = LANES - GDN_HEADS

_NT = (((1,), (1,)), ((), ()))


def _params(*sem):
    return pltpu.CompilerParams(dimension_semantics=sem, vmem_limit_bytes=VMEM_LIMIT)


def _dot(a, b):
    return jnp.dot(a, b, preferred_element_type=F32)


def _dot_nt(a, b):
    return lax.dot_general(a, b, _NT, preferred_element_type=F32)


def _softplus(x):
    return jnp.maximum(x, 0.0) + jnp.log1p(jnp.exp(-jnp.abs(x)))


def _softplus2(x):
    return jnp.maximum(x, 0.0) + jnp.log2(1.0 + jnp.exp2(-jnp.abs(x)))


def _sigmoid(x):
    return 1.0 / (1.0 + jnp.exp(-x))


def _row_groups(rows):
    sub = min(SUB_ROWS, rows)
    return [slice(r, r + sub) for r in range(0, rows, sub)]


def _split2(x):
    hi = x.astype(BF16)
    lo = (x - hi.astype(F32)).astype(BF16)
    return hi, lo


def _split3(x):
    h1 = x.astype(BF16)
    r1 = x - h1.astype(F32)
    h2 = r1.astype(BF16)
    h3 = (r1 - h2.astype(F32)).astype(BF16)
    return h1, h2, h3


def _rmsnorm_kernel(x_ref, w_ref, o_ref):
    x = x_ref[...]
    y = x * lax.rsqrt(jnp.mean(x * x, axis=-1, keepdims=True) + EPS)
    o_ref[...] = (y * w_ref[...]).astype(o_ref.dtype)


def _rmsnorm(x, w, out_dtype, tm):
    m, d = x.shape
    return pl.pallas_call(
        _rmsnorm_kernel,
        grid=(m // tm,),
        in_specs=[pl.BlockSpec((tm, d), lambda i: (i, 0)),
                  pl.BlockSpec((1, d), lambda i: (0, 0))],
        out_specs=pl.BlockSpec((tm, d), lambda i: (i, 0)),
        out_shape=jax.ShapeDtypeStruct((m, d), out_dtype),
        compiler_params=_params("parallel"),
        name="rmsnorm",
    )(x, w.reshape(1, d))


def _proj_kernel(a_ref, w_ref, o_ref, *rest, head_major_blocks):
    *head_major_refs, wbf_ref = rest

    @pl.when(pl.program_id(1) == 0)
    def _():
        wbf_ref[...] = w_ref[...].astype(BF16)

    o_ref[...] = _dot(a_ref[...], wbf_ref[...])

    rows = o_ref.shape[0]
    for blk, ref in zip(head_major_blocks, head_major_refs):
        @pl.when(pl.program_id(0) == blk)
        def _(ref=ref):
            for h in range(SB_HEADS):
                ref[pl.ds(h, rows, stride=SB_HEADS), :] = o_ref[:, h * HEAD_DIM:(h + 1) * HEAD_DIM]


def _many_rows(m):
    return m % WIDE_TILE == 0


def _weight_spec(shape, index_map, m):
    return pl.BlockSpec(shape, index_map, pipeline_mode=pl.Buffered(1 if _many_rows(m) else 2))


def _proj(a, w_in):
    m, d = a.shape
    tm, tn = (WIDE_TILE, WIDE_TILE) if _many_rows(m) else (m, COL_TILE)
    n_m = m // tm
    head_major = tn == SB_WIDTH
    out_specs = [pl.BlockSpec((tm, tn), lambda j, i: (i, j))]
    out_shape = [jax.ShapeDtypeStruct((m, PROJ_MAIN), F32)]
    blocks = (SB_WIDTH // tn, 2 * SB_WIDTH // tn) if head_major else ()
    for blk in blocks:
        out_specs.append(pl.BlockSpec(
            (tm * SB_HEADS, HEAD_DIM),
            lambda j, i, blk=blk: (jnp.where(j == blk, i, jnp.where(j < blk, 0, n_m - 1)), 0)))
        out_shape.append(jax.ShapeDtypeStruct((m * SB_HEADS, HEAD_DIM), F32))
    proj, *kv = pl.pallas_call(
        functools.partial(_proj_kernel, head_major_blocks=blocks),
        grid=(PROJ_MAIN // tn, n_m),
        in_specs=[pl.BlockSpec((tm, d), lambda j, i: (i, 0)),
                  _weight_spec((d, tn), lambda j, i: (0, j), m)],
        out_specs=out_specs,
        out_shape=out_shape,
        scratch_shapes=[pltpu.VMEM((d, tn), BF16)],
        compiler_params=_params("arbitrary", "arbitrary"),
        name="proj",
    )(a, w_in)
    if not head_major:
        kv = [proj[:, SB_WIDTH:2 * SB_WIDTH], proj[:, 2 * SB_WIDTH:3 * SB_WIDTH]]
    sb_k, sb_v = (x.reshape(m, SB_HEADS, HEAD_DIM) for x in kv)
    return proj, sb_k, sb_v


def _gate_proj_kernel(a_ref, w_ref, wt_ref, ab_ref, abt_ref):
    a = a_ref[...]
    ab_ref[...] = _dot(a, w_ref[...])
    abt_ref[...] = _dot_nt(wt_ref[...], a)


def _gate_proj(a, w_tail, tm):
    m, d = a.shape
    w = w_tail.astype(BF16)
    return pl.pallas_call(
        _gate_proj_kernel,
        grid=(m // tm,),
        in_specs=[pl.BlockSpec((tm, d), lambda i: (i, 0)),
                  pl.BlockSpec((d, LANES), lambda i: (0, 0)),
                  pl.BlockSpec((LANES, d), lambda i: (0, 0))],
        out_specs=[pl.BlockSpec((tm, LANES), lambda i: (i, 0)),
                   pl.BlockSpec((LANES, tm), lambda i: (0, i))],
        out_shape=[jax.ShapeDtypeStruct((m, LANES), F32),
                   jax.ShapeDtypeStruct((LANES, m), F32)],
        compiler_params=_params("parallel"),
        name="gate_proj",
    )(a, w, w.T)


def _cumsum_weights():
    j = lax.broadcasted_iota(jnp.int32, (LANES, LANES), 0)
    s = lax.broadcasted_iota(jnp.int32, (LANES, LANES), 1)
    incl = (j >= s).astype(BF16)
    half = jnp.concatenate([incl, jnp.ones((LANES, LANES), BF16)], axis=1)
    return jnp.concatenate([half, half], axis=0)


def _sb_chunk(z, vc, w2, r, diag):
    n = vc.shape[0] // LANES
    blocks = []
    for c in range(n):
        row0 = c * LANES if diag else 0
        zc = z[row0:, c * LANES:(c + 1) * LANES]
        sp = _softplus2(zc)
        valid = None
        if diag:
            valid = (lax.broadcasted_iota(jnp.int32, zc.shape, 1)
                     < lax.broadcasted_iota(jnp.int32, zc.shape, 0))
            sp = jnp.where(valid, sp, 0.0)
        hi, lo = _split2(sp)
        blocks.append((row0, zc, valid, jnp.concatenate([hi, lo], axis=1)))
    res = _dot(jnp.concatenate([blk[3] for blk in blocks], axis=0), w2)
    ends = []
    for blk in blocks:
        ends.append((ends[-1] if ends else 0) + blk[1].shape[0])
    a_parts = [None] * n
    for c in reversed(range(n)):
        row0, zc, valid, _ = blocks[c]
        rc = res[ends[c] - zc.shape[0]:ends[c]]
        a = jnp.exp2(zc - rc[:, :LANES] - r[row0:])
        if diag:
            a = jnp.where(valid, a, 0.0)
        a = a.astype(BF16)
        r_new = r[row0:] + rc[:, LANES:]
        if row0:
            a = jnp.concatenate([jnp.zeros((row0, LANES), BF16), a], axis=0)
            r_new = jnp.concatenate([r[:row0], r_new], axis=0)
        a_parts[c] = a
        r = r_new
    return _dot(jnp.concatenate(a_parts, axis=1), vc), r


def _sb_prompt_kernel(c_ref, q_ref, k_ref, v_ref, w2_ref, nw_ref, o_ref, kbf, vbf, acc, run, zbuf):
    h = pl.program_id(1)
    i = pl.program_id(2)

    @pl.when(i == 0)
    def _():
        kbf[...] = k_ref[...].astype(BF16)
        vbf[...] = v_ref[...].astype(BF16)

    z_bias = c_ref[h] * LOG2_E
    w2 = w2_ref[...]
    q = (q_ref[...] * (SB_SCALE * LOG2_E)).astype(BF16)

    def rows_of(chunk):
        return pl.ds(pl.multiple_of(jnp.maximum(chunk, 0) * SB_TQ, SB_TQ), SB_TQ)

    def logits(chunk):
        return _dot_nt(q, kbf[rows_of(chunk), :]) + z_bias

    zbuf[0] = logits(i - 1)
    d, r = _sb_chunk(logits(i), vbf[rows_of(i), :], w2, jnp.zeros(run.shape, F32), True)
    acc[...] = d
    run[...] = r

    def step(chunk, slot, more):
        if more:
            zbuf[1 - slot] = logits(chunk - 1)
        d, r = _sb_chunk(zbuf[slot], vbf[rows_of(chunk), :], w2, run[...], False)
        acc[...] += d
        run[...] = r

    def body(j, carry):
        step(i - 1 - 2 * j, 0, True)
        step(i - 2 - 2 * j, 1, True)
        return carry

    lax.fori_loop(0, i // 2, body, 0)

    @pl.when(i % 2 == 1)
    def _():
        step(0, 0, False)

    o = acc[...]
    y = o * lax.rsqrt(jnp.mean(o * o, axis=-1, keepdims=True) + EPS) * nw_ref[...]
    o_ref[...] = y.astype(o_ref.dtype)


def _sb_prompt(proj, logit_bias, out_norm, b, t):
    nq = t // SB_TQ
    blk = lambda rows, fn: pl.BlockSpec((rows, HEAD_DIM), fn)
    return pl.pallas_call(
        _sb_prompt_kernel,
        grid=(b, SB_HEADS, nq),
        in_specs=[pl.BlockSpec(memory_space=pltpu.SMEM),
                  blk(SB_TQ, lambda bi, h, i: (bi * nq + i, h)),
                  blk(t, lambda bi, h, i: (bi, SB_HEADS + h)),
                  blk(t, lambda bi, h, i: (bi, 2 * SB_HEADS + h)),
                  pl.BlockSpec((2 * LANES, 2 * LANES), lambda bi, h, i: (0, 0)),
                  pl.BlockSpec((1, HEAD_DIM), lambda bi, h, i: (0, 0))],
        out_specs=blk(SB_TQ, lambda bi, h, i: (bi * nq + i, h)),
        out_shape=jax.ShapeDtypeStruct((b * t, SB_WIDTH), BF16),
        scratch_shapes=[pltpu.VMEM((t, HEAD_DIM), BF16), pltpu.VMEM((t, HEAD_DIM), BF16),
                        pltpu.VMEM((SB_TQ, HEAD_DIM), F32), pltpu.VMEM((SB_TQ, LANES), F32),
                        pltpu.VMEM((2, SB_TQ, SB_TQ), F32)],
        compiler_params=_params("arbitrary", "arbitrary", "arbitrary"),
        name="sb_prompt",
    )(logit_bias, proj, proj, proj, _cumsum_weights(), out_norm.reshape(1, HEAD_DIM))


def _sb_decode_kernel(pt_ref, q_ref, kn_ref, vn_ref, *rest, past_len):
    k_refs = rest[:SB_PAGES_PER_STEP]
    v_refs = rest[SB_PAGES_PER_STEP:2 * SB_PAGES_PER_STEP]
    c_ref, nw_ref, o_ref, acc, run = rest[2 * SB_PAGES_PER_STEP:]
    p = pl.program_id(1)
    page = k_refs[0].shape[1]
    flat = SB_PAGES_PER_STEP * page * SB_HEADS
    lane = lax.broadcasted_iota(jnp.int32, (SB_HEADS, flat), 1)
    own = (lane % SB_HEADS) == lax.broadcasted_iota(jnp.int32, (SB_HEADS, flat), 0)
    z_bias = c_ref[...][:, :1]
    qb = (q_ref[0] * SB_SCALE).astype(BF16)

    @pl.when(p == 0)
    def _():
        kn = kn_ref[0].astype(BF16).astype(F32)
        z = jnp.sum(qb.astype(F32) * kn, axis=-1, keepdims=True) + z_bias
        valid = jnp.full(z.shape, past_len, jnp.int32) < past_len
        sp = jnp.where(valid, _softplus(z), 0.0)
        a = jnp.where(valid, jnp.exp(z - sp), 0.0)
        acc[...] = a.astype(BF16).astype(F32) * vn_ref[0].astype(BF16).astype(F32)
        run[...] = jnp.broadcast_to(sp, run.shape)

    flatten = lambda ref: ref[0].reshape(page * SB_HEADS, HEAD_DIM).astype(BF16)
    kp = jnp.concatenate([flatten(ref) for ref in reversed(k_refs)], axis=0)
    vp = jnp.concatenate([flatten(ref) for ref in reversed(v_refs)], axis=0)
    z = _dot_nt(qb, kp) + z_bias
    sp = jnp.where(own, _softplus(z), 0.0)
    s = sp
    step = SB_HEADS
    while step < flat:
        s = s + jnp.where(lane + step < flat, pltpu.roll(s, flat - step, axis=1), 0.0)
        step *= 2
    r = run[...]
    a = jnp.where(own, jnp.exp(z - s - r[:, :1]), 0.0)
    run[...] = r + jnp.sum(sp, axis=1, keepdims=True)
    acc[...] += _dot(a.astype(BF16), vp)

    @pl.when(p == pl.num_programs(1) - 1)
    def _():
        o = acc[...]
        y = o * lax.rsqrt(jnp.mean(o * o, axis=-1, keepdims=True) + EPS) * nw_ref[...]
        o_ref[0] = y.astype(o_ref.dtype)


def _sb_decode(proj, cache_k, cache_v, page_table, logit_bias, out_norm):
    b = proj.shape[0]
    _, page, _, _ = cache_k.shape
    n_pages = page_table.shape[1]
    pps = SB_PAGES_PER_STEP
    assert n_pages % pps == 0
    proj3 = proj.reshape(b, PROJ_MAIN // HEAD_DIM, HEAD_DIM)
    row = lambda blk: pl.BlockSpec((1, SB_HEADS, HEAD_DIM), lambda bi, p, pt: (bi, blk, 0))
    pg = lambda s: pl.BlockSpec(
        (1, page, SB_HEADS, HEAD_DIM),
        lambda bi, p, pt: (pt[bi * n_pages + n_pages - 1 - (p * pps + s)], 0, 0, 0))
    pages = [pg(s) for s in range(pps)]
    const = lambda shape: pl.BlockSpec(shape, lambda bi, p, pt: (0,) * len(shape))
    grid_spec = pltpu.PrefetchScalarGridSpec(
        num_scalar_prefetch=1,
        grid=(b, n_pages // pps),
        in_specs=[row(0), row(1), row(2), *pages, *pages,
                  const((SB_HEADS, LANES)), const((1, HEAD_DIM))],
        out_specs=pl.BlockSpec((1, SB_HEADS, HEAD_DIM), lambda bi, p, pt: (bi, 0, 0)),
        scratch_shapes=[pltpu.VMEM((SB_HEADS, HEAD_DIM), F32),
                        pltpu.VMEM((SB_HEADS, LANES), F32)],
    )
    out = pl.pallas_call(
        functools.partial(_sb_decode_kernel, past_len=n_pages * page),
        grid_spec=grid_spec,
        out_shape=jax.ShapeDtypeStruct((b, SB_HEADS, HEAD_DIM), BF16),
        compiler_params=_params("arbitrary", "arbitrary"),
        name="sb_decode",
    )(page_table.reshape(-1), proj3, proj3, proj3, *([cache_k] * pps), *([cache_v] * pps),
      jnp.broadcast_to(logit_bias[:, None], (SB_HEADS, LANES)), out_norm.reshape(1, HEAD_DIM))
    return out.reshape(b, SB_WIDTH)


def _shift_rows(x, prev, k):
    xr = pltpu.roll(x, k, axis=0)
    row = lax.broadcasted_iota(jnp.int32, prev.shape, 0)
    top = jnp.where(row < k, pltpu.roll(prev, k, axis=0), xr[:SUBLANES])
    return jnp.concatenate([top, xr[SUBLANES:]], axis=0)


def _gdn_kernel(cin_ref, z_ref, ab_ref, abt_ref, cst_ref, s0_ref, cw_ref, prow_ref, pcol_ref,
                nw_ref, o_ref, sfin_ref, s_scr, hist_scr, *, t_valid, t_total):
    c_len = GDN_C
    t = pl.program_id(1)

    @pl.when(t == 0)
    def _():
        s_scr[...] = s0_ref[0]
        hist_scr[...] = cst_ref[0]

    x = cin_ref[...]
    hist = hist_scr[...]
    cw = cw_ref[...]
    y = x * cw[3:4]
    for kk in range(1, GDN_CONV):
        y = y + _shift_rows(x, hist, kk) * cw[GDN_CONV - 1 - kk:GDN_CONV - kk]
    hist_scr[...] = x[c_len - SUBLANES:]
    conv = y * _sigmoid(y)

    ri = lax.broadcasted_iota(jnp.int32, (c_len, c_len), 0)
    ci = lax.broadcasted_iota(jnp.int32, (c_len, c_len), 1)
    tri = ri >= ci
    strict = ri > ci
    eye = (ri == ci).astype(F32)
    eye_bf = eye.astype(BF16)
    ltri = tri.astype(BF16)
    utri = (ri <= ci).astype(BF16)

    ab = ab_ref[...]
    g_col = -jnp.exp(prow_ref[0:1]) * _softplus(ab + prow_ref[1:2])
    beta_col = _sigmoid(ab)
    abt = abt_ref[0]
    g_row = (-jnp.exp(pcol_ref[0:GDN_HEADS]) *
             _softplus(abt[0:GDN_HEADS] + pcol_ref[GDN_HEADS:2 * GDN_HEADS]))
    if t_valid < t_total:
        pos_c = t * c_len + lax.broadcasted_iota(jnp.int32, (c_len, LANES), 0)
        g_col = jnp.where(pos_c < t_valid, g_col, 0.0)
        beta_col = jnp.where(pos_c < t_valid, beta_col, 0.0)
        pos_r = t * c_len + lax.broadcasted_iota(jnp.int32, (GDN_HEADS, c_len), 1)
        g_row = jnp.where(pos_r < t_valid, g_row, 0.0)
    gc_col = sum(_dot(ltri, part) for part in _split3(g_col))
    gc_row = sum(_dot(part, utri) for part in _split3(g_row))

    levels = []
    m = 1
    while m < c_len:
        lm = m.bit_length() - 1
        same = ((ri ^ ci) >> (lm + 1)) == 0
        lower = ((ri >> lm) & 1) > ((ci >> lm) & 1)
        levels.append(jnp.where(same, jnp.where(lower, 1.0, 0.0), 0.0))
        m *= 2

    heads = range(GDN_HEADS)
    head_cols = lambda base, h: slice(base + h * HEAD_DIM, base + (h + 1) * HEAD_DIM)
    qn, kn, kn_bf, kb, vb, decay, egc, gcc = [], [], [], [], [], [], [], []
    for h in heads:
        q = conv[:, head_cols(0, h)]
        k = conv[:, head_cols(GDN_WIDTH, h)]
        v = conv[:, head_cols(2 * GDN_WIDTH, h)]
        qn.append(q * lax.rsqrt(jnp.sum(q * q, axis=-1, keepdims=True) + 1e-6) * (HEAD_DIM ** -0.5))
        kn.append(k * lax.rsqrt(jnp.sum(k * k, axis=-1, keepdims=True) + 1e-6))
        kn_bf.append(kn[h].astype(BF16))
        gcc.append(gc_col[:, A_LANE + h:A_LANE + h + 1])
        beta = beta_col[:, B_LANE + h:B_LANE + h + 1]
        gcr = gc_row[h:h + 1, :]
        kb.append(kn[h] * beta)
        vb.append(v * beta)
        decay.append(jnp.where(tri, jnp.exp(jnp.where(tri, gcc[h] - gcr, 0.0)), 0.0))
        egc.append(jnp.exp(gcc[h]))
    low = [jnp.where(strict, _dot_nt(kb[h].astype(BF16), kn_bf[h]) * decay[h], 0.0) for h in heads]
    qk = [(_dot_nt(qn[h].astype(BF16), kn_bf[h]) * decay[h]).astype(BF16) for h in heads]

    xinv = [eye - low[h] * levels[0] for h in heads]
    for lvl in levels[1:]:
        xb = [xinv[h].astype(BF16) for h in heads]
        y1 = [_dot(xb[h], (low[h] * lvl).astype(BF16)).astype(BF16) for h in heads]
        xinv = [xinv[h] - _dot(y1[h], xb[h]) for h in heads]

    sol = []
    for h in heads:
        rhs = jnp.concatenate([vb[h], kb[h] * egc[h]], axis=1)
        sol.append(rhs + _dot((xinv[h] - eye).astype(BF16), rhs.astype(BF16)))
    s_old = [s_scr[h] for h in heads]
    ws_qs = [_dot(jnp.concatenate([sol[h][:, HEAD_DIM:], qn[h] * egc[h]], axis=0).astype(BF16),
                  s_old[h].astype(BF16)) for h in heads]
    v_new = [(sol[h][:, :HEAD_DIM] - ws_qs[h][:c_len]).astype(BF16) for h in heads]
    out = [ws_qs[h][c_len:] + _dot(qk[h], v_new[h]) for h in heads]
    nw = nw_ref[...]
    for h in heads:
        g_last = gcc[h][c_len - 1:c_len, :]
        kd = (kn[h] * jnp.exp(g_last - gcc[h])).astype(BF16)
        kd_t = _dot_nt(eye_bf, kd).astype(BF16)
        s_scr[h] = s_old[h] * jnp.exp(g_last) + _dot(kd_t, v_new[h])
    for h in heads:
        zg = z_ref[:, head_cols(0, h)]
        o = out[h]
        on = o * lax.rsqrt(jnp.mean(o * o, axis=-1, keepdims=True) + EPS) * nw
        o_ref[:, head_cols(0, h)] = (on * (zg * _sigmoid(zg))).astype(o_ref.dtype)

    @pl.when(t == pl.num_programs(1) - 1)
    def _():
        sfin_ref[0] = s_scr[...]


def _gdn(src, cin_blk, z_blk, ab, abt, conv_state, rec_state, conv_w, a_log, dt_bias, out_norm,
         b, t_pad, t_valid):
    c_len = GDN_C
    nt = t_pad // c_len
    cst = jnp.pad(conv_state, ((0, 0), (SUBLANES - (GDN_CONV - 1), 0), (0, 0)))
    cw = jnp.pad(conv_w, ((0, SUBLANES - GDN_CONV), (0, 0)))
    prow = jnp.zeros((SUBLANES, LANES), F32)
    prow = prow.at[0, A_LANE:B_LANE].set(a_log).at[1, A_LANE:B_LANE].set(dt_bias)
    pcol = jnp.broadcast_to(jnp.concatenate([a_log, dt_bias])[:, None], (2 * GDN_HEADS, LANES))
    abt3 = abt[A_LANE:].reshape(2 * GDN_HEADS, b * nt, c_len).transpose(1, 0, 2)
    full = lambda shape: pl.BlockSpec(shape, lambda bi, ti: (0,) * len(shape))
    return pl.pallas_call(
        functools.partial(_gdn_kernel, t_valid=t_valid, t_total=t_pad),
        grid=(b, nt),
        in_specs=[pl.BlockSpec((c_len, GDN_CONV_CH), lambda bi, ti: (bi * nt + ti, cin_blk)),
                  pl.BlockSpec((c_len, GDN_WIDTH), lambda bi, ti: (bi * nt + ti, z_blk)),
                  pl.BlockSpec((c_len, LANES), lambda bi, ti: (bi * nt + ti, 0)),
                  pl.BlockSpec((1, 2 * GDN_HEADS, c_len), lambda bi, ti: (bi * nt + ti, 0, 0)),
                  pl.BlockSpec((1, SUBLANES, GDN_CONV_CH), lambda bi, ti: (bi, 0, 0)),
                  pl.BlockSpec((1, GDN_HEADS, HEAD_DIM, HEAD_DIM), lambda bi, ti: (bi, 0, 0, 0)),
                  full((SUBLANES, GDN_CONV_CH)), full((SUBLANES, LANES)),
                  full((2 * GDN_HEADS, LANES)), full((1, HEAD_DIM))],
        out_specs=[pl.BlockSpec((c_len, GDN_WIDTH), lambda bi, ti: (bi * nt + ti, 0)),
                   pl.BlockSpec((1, GDN_HEADS, HEAD_DIM, HEAD_DIM), lambda bi, ti: (bi, 0, 0, 0))],
        out_shape=[jax.ShapeDtypeStruct((b * t_pad, GDN_WIDTH), BF16),
                   jax.ShapeDtypeStruct((b, GDN_HEADS, HEAD_DIM, HEAD_DIM), F32)],
        scratch_shapes=[pltpu.VMEM((GDN_HEADS, HEAD_DIM, HEAD_DIM), F32),
                        pltpu.VMEM((SUBLANES, GDN_CONV_CH), F32)],
        compiler_params=_params("arbitrary", "arbitrary"),
        name="gdn",
    )(src, src, ab, abt3, cst, rec_state, cw, prow, pcol, out_norm.reshape(1, HEAD_DIM))


def _rms(x, w):
    return x * lax.rsqrt(jnp.mean(x * x, axis=-1, keepdims=True) + EPS) * w


def _outproj_kernel(ms_ref, mg_ref, w0_ref, w1_ref, x_ref, nw_ref, h_ref, f_ref, w0_bf, w1_bf):
    @pl.when(pl.program_id(0) == 0)
    def _():
        w0_bf[...] = w0_ref[...].astype(BF16)
        w1_bf[...] = w1_ref[...].astype(BF16)

    for rows in _row_groups(x_ref.shape[0]):
        h = (x_ref[rows, :] + _dot(ms_ref[rows, :], w0_bf[...])
             + _dot(mg_ref[rows, :], w1_bf[...]))
        h_ref[rows, :] = h
        f_ref[rows, :] = _rms(h, nw_ref[...]).astype(f_ref.dtype)


def _outproj(mix_sb, mix_gdn, w_out, x, ffn_norm):
    m, d = x.shape
    tm = min(ROW_TILE, m)
    return pl.pallas_call(
        _outproj_kernel,
        grid=(m // tm,),
        in_specs=[pl.BlockSpec((tm, SB_WIDTH), lambda i: (i, 0)),
                  pl.BlockSpec((tm, GDN_WIDTH), lambda i: (i, 0)),
                  pl.BlockSpec((SB_WIDTH, d), lambda i: (0, 0), pipeline_mode=pl.Buffered(1)),
                  pl.BlockSpec((GDN_WIDTH, d), lambda i: (1, 0), pipeline_mode=pl.Buffered(1)),
                  pl.BlockSpec((tm, d), lambda i: (i, 0)),
                  pl.BlockSpec((1, d), lambda i: (0, 0))],
        out_specs=[pl.BlockSpec((tm, d), lambda i: (i, 0)), pl.BlockSpec((tm, d), lambda i: (i, 0))],
        out_shape=[jax.ShapeDtypeStruct((m, d), F32), jax.ShapeDtypeStruct((m, d), BF16)],
        scratch_shapes=[pltpu.VMEM((SB_WIDTH, d), BF16), pltpu.VMEM((GDN_WIDTH, d), BF16)],
        compiler_params=_params("arbitrary"),
        name="outproj",
    )(mix_sb, mix_gdn, w_out, w_out, x, ffn_norm.reshape(1, d))


def _ffn_up_kernel(f_ref, wg_ref, wu_ref, cw_ref, hist_ref, act_ref, st_ref, wg_bf, wu_bf, carry,
                   *, tiles_per_seq, single_token):
    i = pl.program_id(1)

    @pl.when(i == 0)
    def _():
        wg_bf[...] = wg_ref[...].astype(BF16)
        wu_bf[...] = wu_ref[...].astype(BF16)

    cw = cw_ref[...]
    if single_token:
        f = f_ref[...]
        g = _dot(f, wg_bf[...])
        u = _dot(f, wu_bf[...])
        h0 = hist_ref[0]
        h1 = hist_ref[1]
        gate = h0 * cw[0:1] + h1 * cw[1:2] + g * cw[2:3]
        st_ref[0] = h1
        st_ref[1] = g
        act_ref[...] = (gate * _sigmoid(gate) * u).astype(act_ref.dtype)
    else:
        @pl.when(i % tiles_per_seq == 0)
        def _():
            carry[...] = hist_ref[0]

        prev = carry[...]
        for rows in _row_groups(f_ref.shape[0]):
            sub = rows.stop - rows.start
            f = f_ref[rows, :]
            g = _dot(f, wg_bf[...])
            u = _dot(f, wu_bf[...])
            gate = (_shift_rows(g, prev, 2) * cw[0:1] + _shift_rows(g, prev, 1) * cw[1:2]
                    + g * cw[2:3])
            act_ref[rows, :] = (gate * _sigmoid(gate) * u).astype(act_ref.dtype)
            prev = g[sub - SUBLANES:]
        carry[...] = prev
        st_ref[0] = prev


def _ffn_up(f, w_gate, w_up, conv_w, state, seq_len):
    m, d = f.shape
    dff = w_gate.shape[1]
    tm = WIDE_TILE if _many_rows(m) else m
    tn = COL_TILE
    nj = pl.cdiv(dff, tn)
    cw = jnp.pad(conv_w, ((0, SUBLANES - FFN_CONV), (0, 0)))
    single = seq_len == 1
    nb = m // seq_len
    if single:
        assert tm == m
        hist = state.transpose(1, 0, 2)
        hist_spec = pl.BlockSpec((FFN_CONV - 1, m, tn), lambda j, i: (0, 0, j))
        st_spec = pl.BlockSpec((FFN_CONV - 1, m, tn), lambda j, i: (0, 0, j))
        st_shape = jax.ShapeDtypeStruct((FFN_CONV - 1, m, dff), F32)
        tiles_per_seq = 1
    else:
        tiles_per_seq = seq_len // tm
        hist = jnp.pad(state, ((0, 0), (SUBLANES - (FFN_CONV - 1), 0), (0, 0)))
        hist_spec = pl.BlockSpec((1, SUBLANES, tn), lambda j, i: (i // tiles_per_seq, 0, j))
        st_spec = pl.BlockSpec((1, SUBLANES, tn), lambda j, i: (i // tiles_per_seq, 0, j))
        st_shape = jax.ShapeDtypeStruct((nb, SUBLANES, dff), F32)
    act, st = pl.pallas_call(
        functools.partial(_ffn_up_kernel, tiles_per_seq=tiles_per_seq, single_token=single),
        grid=(nj, m // tm),
        in_specs=[pl.BlockSpec((tm, d), lambda j, i: (i, 0)),
                  pl.BlockSpec((d, tn), lambda j, i: (0, j)),
                  pl.BlockSpec((d, tn), lambda j, i: (0, j)),
                  pl.BlockSpec((SUBLANES, tn), lambda j, i: (0, j)),
                  hist_spec],
        out_specs=[pl.BlockSpec((tm, tn), lambda j, i: (i, j)), st_spec],
        out_shape=[jax.ShapeDtypeStruct((m, dff), BF16), st_shape],
        scratch_shapes=[pltpu.VMEM((d, tn), BF16), pltpu.VMEM((d, tn), BF16),
                        pltpu.VMEM((SUBLANES, tn), F32)],
        compiler_params=_params("arbitrary", "arbitrary"),
        name="ffn_up",
    )(f, w_gate, w_up, cw, hist)
    if single:
        return act, st.transpose(1, 0, 2)
    return act, st[:, SUBLANES - (FFN_CONV - 1):]


def _ffn_down_kernel(act_ref, w_ref, h_ref, o_ref, w_bf):
    @pl.when(pl.program_id(1) == 0)
    def _():
        w_bf[...] = w_ref[...].astype(BF16)

    o_ref[...] = h_ref[...] + _dot(act_ref[...], w_bf[...])


def _ffn_down(act, w_down, h):
    m, dff = act.shape
    d = h.shape[1]
    tm, tn = (ROW_TILE, WIDE_TILE) if _many_rows(m) else (m, COL_TILE)
    return pl.pallas_call(
        _ffn_down_kernel,
        grid=(d // tn, m // tm),
        in_specs=[pl.BlockSpec((tm, dff), lambda j, i: (i, 0)),
                  _weight_spec((dff, tn), lambda j, i: (0, j), m),
                  pl.BlockSpec((tm, tn), lambda j, i: (i, j))],
        out_specs=pl.BlockSpec((tm, tn), lambda j, i: (i, j)),
        out_shape=jax.ShapeDtypeStruct((m, d), F32),
        scratch_shapes=[pltpu.VMEM((dff, tn), BF16)],
        compiler_params=_params("arbitrary", "arbitrary"),
        name="ffn_down",
    )(act, w_down, h)


def _ple_kernel(r_ref, p_ref, wg_ref, wp_ref, h_ref, nw_ref, o_ref, wg_bf, wp_bf):
    @pl.when(pl.program_id(0) == 0)
    def _():
        wg_bf[...] = wg_ref[...].astype(BF16)
        wp_bf[...] = wp_ref[...].astype(BF16)

    for rows in _row_groups(h_ref.shape[0]):
        gate = _sigmoid(_dot(r_ref[rows, :], wg_bf[...]))
        emb = _dot(p_ref[rows, :].astype(BF16), wp_bf[...])
        o_ref[rows, :] = _rms(h_ref[rows, :] + emb * gate, nw_ref[...])


def _ple(r, p, w_gate, w_proj, h, final_norm):
    m, d = h.shape
    pd = p.shape[1]
    tm = min(ROW_TILE, m)
    return pl.pallas_call(
        _ple_kernel,
        grid=(m // tm,),
        in_specs=[pl.BlockSpec((tm, d), lambda i: (i, 0)),
                  pl.BlockSpec((tm, pd), lambda i: (i, 0)),
                  pl.BlockSpec((d, d), lambda i: (0, 0), pipeline_mode=pl.Buffered(1)),
                  pl.BlockSpec((pd, d), lambda i: (0, 0), pipeline_mode=pl.Buffered(1)),
                  pl.BlockSpec((tm, d), lambda i: (i, 0)),
                  pl.BlockSpec((1, d), lambda i: (0, 0))],
        out_specs=pl.BlockSpec((tm, d), lambda i: (i, 0)),
        out_shape=jax.ShapeDtypeStruct((m, d), F32),
        scratch_shapes=[pltpu.VMEM((d, d), BF16), pltpu.VMEM((pd, d), BF16)],
        compiler_params=_params("arbitrary"),
        name="ple",
    )(r, p, w_gate, w_proj, h, final_norm.reshape(1, d))


def _layer(x, p, sb_fn, gdn_conv_state, gdn_rec_state, ffn_conv_state, weights, final_norm):
    (attn_norm, w_in, sb_out_norm, gdn_conv_w, gdn_a_log, gdn_dt_bias, gdn_out_norm, w_out,
     ffn_norm, w_ffn_gate, w_ffn_up, ffn_conv_w, w_ffn_down, ple_norm, w_ple_gate,
     w_ple_proj) = weights
    b, t, d = x.shape
    m = b * t
    tm = min(ROW_TILE, m)
    x2 = x.reshape(m, d)

    a = _rmsnorm(x2, attn_norm, BF16, tm)
    proj, sb_k, sb_v = _proj(a, w_in)
    ab, abt = _gate_proj(a, w_in[:, w_in.shape[1] - LANES:], tm)
    mix_sb = sb_fn(proj)

    conv_in = proj[:, O_GDN_QKV:O_GDN_Z].reshape(b, t, GDN_CONV_CH)
    if t % GDN_C == 0:
        mix_gdn, rec = _gdn(proj, O_GDN_QKV // GDN_CONV_CH, O_GDN_Z // GDN_WIDTH, ab, abt,
                            gdn_conv_state, gdn_rec_state, gdn_conv_w, gdn_a_log, gdn_dt_bias,
                            gdn_out_norm, b, t, t)
    else:
        t_pad = -(-t // GDN_C) * GDN_C
        pad_rows = lambda arr: jnp.pad(arr.reshape(b, t, -1), ((0, 0), (0, t_pad - t), (0, 0))
                                       ).reshape(b * t_pad, -1)
        src = pad_rows(proj[:, O_GDN_QKV:])
        abt_pad = jnp.pad(abt.reshape(LANES, b, t), ((0, 0), (0, 0), (0, t_pad - t))
                          ).reshape(LANES, b * t_pad)
        mix_gdn, rec = _gdn(src, 0, GDN_CONV_CH // GDN_WIDTH, pad_rows(ab), abt_pad,
                            gdn_conv_state, gdn_rec_state, gdn_conv_w, gdn_a_log, gdn_dt_bias,
                            gdn_out_norm, b, t_pad, t)
        mix_gdn = mix_gdn.reshape(b, t_pad, GDN_WIDTH)[:, :t].reshape(m, GDN_WIDTH)
    new_gdn_conv = jnp.concatenate([gdn_conv_state, conv_in], axis=1)[:, t:]

    h, f = _outproj(mix_sb, mix_gdn, w_out, x2, ffn_norm)
    act, new_ffn_conv = _ffn_up(f, w_ffn_gate, w_ffn_up, ffn_conv_w, ffn_conv_state, t)
    h = _ffn_down(act, w_ffn_down, h)
    r = _rmsnorm(h, ple_norm, BF16, tm)
    y = _ple(r, p.reshape(m, -1), w_ple_gate, w_ple_proj, h, final_norm)

    sb_k = sb_k.reshape(b, t, SB_HEADS, HEAD_DIM)
    sb_v = sb_v.reshape(b, t, SB_HEADS, HEAD_DIM)
    return y.reshape(b, t, d), sb_k, sb_v, new_gdn_conv, rec, new_ffn_conv


def kernel(x_prompt, x_sample, cache_sb_k, cache_sb_v, page_table, state_gdn_conv, state_gdn_rec, state_ffn_conv, p_prompt, p_sample, attn_norm, w_in, sb_logit_bias, sb_out_norm, gdn_conv_w, gdn_a_log, gdn_dt_bias, gdn_out_norm, w_out, ffn_norm, w_ffn_gate, w_ffn_up, ffn_conv_w, w_ffn_down, ple_norm, w_ple_gate, w_ple_proj, final_norm):
    assert attn_norm.shape[0] == 1, "single-layer step"
    assert x_sample.shape[1] == 1, "decode group carries one token per sequence"
    bp, tp, _ = x_prompt.shape
    bs = x_sample.shape[0]
    weights = (attn_norm[0], w_in[0], sb_out_norm[0], gdn_conv_w[0], gdn_a_log[0], gdn_dt_bias[0],
               gdn_out_norm[0], w_out[0], ffn_norm[0], w_ffn_gate[0], w_ffn_up[0], ffn_conv_w[0],
               w_ffn_down[0], ple_norm[0], w_ple_gate[0], w_ple_proj[0])
    bias = sb_logit_bias[0]
    out_norm = sb_out_norm[0]

    zeros = lambda *shape: jnp.zeros(shape, x_prompt.dtype)
    out_p = _layer(
        x_prompt, p_prompt[0],
        lambda proj: _sb_prompt(proj, bias, out_norm, bp, tp),
        zeros(bp, GDN_CONV - 1, GDN_CONV_CH), zeros(bp, GDN_HEADS, HEAD_DIM, HEAD_DIM),
        zeros(bp, FFN_CONV - 1, w_ffn_gate.shape[2]), weights, final_norm)
    out_s = _layer(
        x_sample, p_sample[0],
        lambda proj: _sb_decode(proj, cache_sb_k[0], cache_sb_v[0], page_table, bias, out_norm),
        state_gdn_conv[0], state_gdn_rec[0], state_ffn_conv[0], weights, final_norm)

    (y_p, *st_p), (y_s, *st_s) = out_p, out_s
    return (y_p, y_s, *[s[None] for s in st_p], *[s[None] for s in st_s])
```

```python
import functools

import jax
import jax.numpy as jnp
from jax import lax
from jax.experimental import pallas as pl
from jax.experimental.pallas import tpu as pltpu

F32 = jnp.float32
BF16 = jnp.bfloat16

EPS = 1e-6
HEAD_DIM = 128
SB_HEADS = 8
GDN_HEADS = 8
SB_WIDTH = SB_HEADS * HEAD_DIM
GDN_WIDTH = GDN_HEADS * HEAD_DIM
GDN_CONV = 4
GDN_CONV_CH = 3 * GDN_WIDTH
FFN_CONV = 3
SB_SCALE = HEAD_DIM ** -0.5
LOG2_E = 1.4426950408889634
O_GDN_QKV = 3 * SB_WIDTH
O_GDN_Z = O_GDN_QKV + GDN_CONV_CH
PROJ_MAIN = O_GDN_Z + GDN_WIDTH

LANES = 128
SUBLANES = 8
VMEM_LIMIT = 56 * 1024 * 1024
COL_TILE = 512
ROW_TILE = 512
WIDE_TILE = 1024
SUB_ROWS = 256
SB_TQ = 512
SB_PAGES_PER_STEP = 8
GDN_C = 128
A_LANE = LANES - 2 * GDN_HEADS
B_LANE = LANES - GDN_HEADS

_NT = (((1,), (1,)), ((), ()))


def _params(*sem):
    return pltpu.CompilerParams(dimension_semantics=sem, vmem_limit_bytes=VMEM_LIMIT)


def _dot(a, b):
    return jnp.dot(a, b, preferred_element_type=F32)


def _dot_nt(a, b):
    return lax.dot_general(a, b, _NT, preferred_element_type=F32)


def _softplus(x):
    return jnp.maximum(x, 0.0) + jnp.log1p(jnp.exp(-jnp.abs(x)))


def _softplus2(x):
    return jnp.maximum(x, 0.0) + jnp.log2(1.0 + jnp.exp2(-jnp.abs(x)))


def _sigmoid(x):
    return 1.0 / (1.0 + jnp.exp(-x))


def _row_groups(rows):
    sub = min(SUB_ROWS, rows)
    return [slice(r, r + sub) for r in range(0, rows, sub)]


def _split2(x):
    hi = x.astype(BF16)
    lo = (x - hi.astype(F32)).astype(BF16)
    return hi, lo


def _split3(x):
    h1 = x.astype(BF16)
    r1 = x - h1.astype(F32)
    h2 = r1.astype(BF16)
    h3 = (r1 - h2.astype(F32)).astype(BF16)
    return h1, h2, h3


def _rmsnorm_kernel(x_ref, w_ref, o_ref):
    x = x_ref[...]
    y = x * lax.rsqrt(jnp.mean(x * x, axis=-1, keepdims=True) + EPS)
    o_ref[...] = (y * w_ref[...]).astype(o_ref.dtype)


def _rmsnorm(x, w, out_dtype, tm):
    m, d = x.shape
    return pl.pallas_call(
        _rmsnorm_kernel,
        grid=(m // tm,),
        in_specs=[pl.BlockSpec((tm, d), lambda i: (i, 0)),
                  pl.BlockSpec((1, d), lambda i: (0, 0))],
        out_specs=pl.BlockSpec((tm, d), lambda i: (i, 0)),
        out_shape=jax.ShapeDtypeStruct((m, d), out_dtype),
        compiler_params=_params("parallel"),
        name="rmsnorm",
    )(x, w.reshape(1, d))


def _proj_kernel(a_ref, w_ref, wg_ref, wgt_ref, o_ref, ab_ref, abt_ref, *rest, head_major_blocks):
    *head_major_refs, wbf_ref = rest

    @pl.when(pl.program_id(1) == 0)
    def _():
        wbf_ref[...] = w_ref[...].astype(BF16)

    o_ref[...] = _dot(a_ref[...], wbf_ref[...])

    @pl.when(pl.program_id(0) == 0)
    def _():
        a = a_ref[...]
        ab_ref[...] = _dot(a, wg_ref[...])
        abt_ref[...] = _dot_nt(wgt_ref[...], a)

    rows = o_ref.shape[0]
    for blk, ref in zip(head_major_blocks, head_major_refs):
        @pl.when(pl.program_id(0) == blk)
        def _(ref=ref):
            for h in range(SB_HEADS):
                ref[pl.ds(h, rows, stride=SB_HEADS), :] = o_ref[:, h * HEAD_DIM:(h + 1) * HEAD_DIM]


def _many_rows(m):
    return m % WIDE_TILE == 0


def _weight_spec(shape, index_map, m):
    return pl.BlockSpec(shape, index_map, pipeline_mode=pl.Buffered(1 if _many_rows(m) else 2))


def _proj(a, w_in):
    m, d = a.shape
    tm, tn = (WIDE_TILE, WIDE_TILE) if _many_rows(m) else (m, COL_TILE)
    n_m = m // tm
    head_major = tn == SB_WIDTH

    def during(blk):
        return lambda j, i: jnp.where(j == blk, i, jnp.where(j < blk, 0, n_m - 1))

    out_specs = [pl.BlockSpec((tm, tn), lambda j, i: (i, j)),
                 pl.BlockSpec((tm, LANES), lambda j, i: (during(0)(j, i), 0)),
                 pl.BlockSpec((LANES, tm), lambda j, i: (0, during(0)(j, i)))]
    out_shape = [jax.ShapeDtypeStruct((m, PROJ_MAIN), F32),
                 jax.ShapeDtypeStruct((m, LANES), F32),
                 jax.ShapeDtypeStruct((LANES, m), F32)]
    blocks = (SB_WIDTH // tn, 2 * SB_WIDTH // tn) if head_major else ()
    for blk in blocks:
        out_specs.append(pl.BlockSpec((tm * SB_HEADS, HEAD_DIM),
                                      lambda j, i, blk=blk: (during(blk)(j, i), 0)))
        out_shape.append(jax.ShapeDtypeStruct((m * SB_HEADS, HEAD_DIM), F32))
    w_tail = w_in[:, w_in.shape[1] - LANES:].astype(BF16)
    proj, ab, abt, *kv = pl.pallas_call(
        functools.partial(_proj_kernel, head_major_blocks=blocks),
        grid=(PROJ_MAIN // tn, n_m),
        in_specs=[pl.BlockSpec((tm, d), lambda j, i: (i, 0)),
                  _weight_spec((d, tn), lambda j, i: (0, j), m),
                  pl.BlockSpec((d, LANES), lambda j, i: (0, 0)),
                  pl.BlockSpec((LANES, d), lambda j, i: (0, 0))],
        out_specs=out_specs,
        out_shape=out_shape,
        scratch_shapes=[pltpu.VMEM((d, tn), BF16)],
        compiler_params=_params("arbitrary", "arbitrary"),
        name="proj",
    )(a, w_in, w_tail, w_tail.T)
    if not head_major:
        kv = [proj[:, SB_WIDTH:2 * SB_WIDTH], proj[:, 2 * SB_WIDTH:3 * SB_WIDTH]]
    sb_k, sb_v = (x.reshape(m, SB_HEADS, HEAD_DIM) for x in kv)
    return proj, ab, abt, sb_k, sb_v


def _cumsum_weights():
    j = lax.broadcasted_iota(jnp.int32, (LANES, LANES), 0)
    s = lax.broadcasted_iota(jnp.int32, (LANES, LANES), 1)
    incl = (j >= s).astype(BF16)
    half = jnp.concatenate([incl, jnp.ones((LANES, LANES), BF16)], axis=1)
    return jnp.concatenate([half, half], axis=0)


def _sb_chunk(z, vc, w2, r, diag):
    n = vc.shape[0] // LANES
    blocks = []
    for c in range(n):
        row0 = c * LANES if diag else 0
        zc = z[row0:, c * LANES:(c + 1) * LANES]
        sp = _softplus2(zc)
        valid = None
        if diag:
            valid = (lax.broadcasted_iota(jnp.int32, zc.shape, 1)
                     < lax.broadcasted_iota(jnp.int32, zc.shape, 0))
            sp = jnp.where(valid, sp, 0.0)
        hi, lo = _split2(sp)
        blocks.append((row0, zc, valid, jnp.concatenate([hi, lo], axis=1)))
    res = _dot(jnp.concatenate([blk[3] for blk in blocks], axis=0), w2)
    ends = []
    for blk in blocks:
        ends.append((ends[-1] if ends else 0) + blk[1].shape[0])
    a_parts = [None] * n
    for c in reversed(range(n)):
        row0, zc, valid, _ = blocks[c]
        rc = res[ends[c] - zc.shape[0]:ends[c]]
        a = jnp.exp2(zc - rc[:, :LANES] - r[row0:])
        if diag:
            a = jnp.where(valid, a, 0.0)
        a = a.astype(BF16)
        r_new = r[row0:] + rc[:, LANES:]
        if row0:
            a = jnp.concatenate([jnp.zeros((row0, LANES), BF16), a], axis=0)
            r_new = jnp.concatenate([r[:row0], r_new], axis=0)
        a_parts[c] = a
        r = r_new
    return _dot(jnp.concatenate(a_parts, axis=1), vc), r


def _sb_prompt_kernel(c_ref, q_ref, k_ref, v_ref, w2_ref, nw_ref, o_ref, kbf, vbf, acc, run, zbuf):
    h = pl.program_id(1)
    i = pl.program_id(2)

    @pl.when(i == 0)
    def _():
        kbf[...] = k_ref[...].astype(BF16)
        vbf[...] = v_ref[...].astype(BF16)

    z_bias = c_ref[h] * LOG2_E
    w2 = w2_ref[...]
    q = (q_ref[...] * (SB_SCALE * LOG2_E)).astype(BF16)

    def rows_of(chunk):
        return pl.ds(pl.multiple_of(jnp.maximum(chunk, 0) * SB_TQ, SB_TQ), SB_TQ)

    def logits(chunk):
        return _dot_nt(q, kbf[rows_of(chunk), :]) + z_bias

    zbuf[0] = logits(i - 1)
    d, r = _sb_chunk(logits(i), vbf[rows_of(i), :], w2, jnp.zeros(run.shape, F32), True)
    acc[...] = d
    run[...] = r

    def step(chunk, slot, more):
        if more:
            zbuf[1 - slot] = logits(chunk - 1)
        d, r = _sb_chunk(zbuf[slot], vbf[rows_of(chunk), :], w2, run[...], False)
        acc[...] += d
        run[...] = r

    def body(j, carry):
        step(i - 1 - 2 * j, 0, True)
        step(i - 2 - 2 * j, 1, True)
        return carry

    lax.fori_loop(0, i // 2, body, 0)

    @pl.when(i % 2 == 1)
    def _():
        step(0, 0, False)

    o = acc[...]
    y = o * lax.rsqrt(jnp.mean(o * o, axis=-1, keepdims=True) + EPS) * nw_ref[...]
    o_ref[...] = y.astype(o_ref.dtype)


def _sb_prompt(proj, logit_bias, out_norm, b, t):
    nq = t // SB_TQ
    blk = lambda rows, fn: pl.BlockSpec((rows, HEAD_DIM), fn)
    return pl.pallas_call(
        _sb_prompt_kernel,
        grid=(b, SB_HEADS, nq),
        in_specs=[pl.BlockSpec(memory_space=pltpu.SMEM),
                  blk(SB_TQ, lambda bi, h, i: (bi * nq + i, h)),
                  blk(t, lambda bi, h, i: (bi, SB_HEADS + h)),
                  blk(t, lambda bi, h, i: (bi, 2 * SB_HEADS + h)),
                  pl.BlockSpec((2 * LANES, 2 * LANES), lambda bi, h, i: (0, 0)),
                  pl.BlockSpec((1, HEAD_DIM), lambda bi, h, i: (0, 0))],
        out_specs=blk(SB_TQ, lambda bi, h, i: (bi * nq + i, h)),
        out_shape=jax.ShapeDtypeStruct((b * t, SB_WIDTH), BF16),
        scratch_shapes=[pltpu.VMEM((t, HEAD_DIM), BF16), pltpu.VMEM((t, HEAD_DIM), BF16),
                        pltpu.VMEM((SB_TQ, HEAD_DIM), F32), pltpu.VMEM((SB_TQ, LANES), F32),
                        pltpu.VMEM((2, SB_TQ, SB_TQ), F32)],
        compiler_params=_params("arbitrary", "arbitrary", "arbitrary"),
        name="sb_prompt",
    )(logit_bias, proj, proj, proj, _cumsum_weights(), out_norm.reshape(1, HEAD_DIM))


def _sb_decode_kernel(pt_ref, q_ref, kn_ref, vn_ref, *rest, past_len):
    k_refs = rest[:SB_PAGES_PER_STEP]
    v_refs = rest[SB_PAGES_PER_STEP:2 * SB_PAGES_PER_STEP]
    c_ref, nw_ref, o_ref, acc, run = rest[2 * SB_PAGES_PER_STEP:]
    p = pl.program_id(1)
    page = k_refs[0].shape[1]
    flat = SB_PAGES_PER_STEP * page * SB_HEADS
    lane = lax.broadcasted_iota(jnp.int32, (SB_HEADS, flat), 1)
    own = (lane % SB_HEADS) == lax.broadcasted_iota(jnp.int32, (SB_HEADS, flat), 0)
    z_bias = c_ref[...][:, :1]
    qb = (q_ref[0] * SB_SCALE).astype(BF16)

    @pl.when(p == 0)
    def _():
        kn = kn_ref[0].astype(BF16).astype(F32)
        z = jnp.sum(qb.astype(F32) * kn, axis=-1, keepdims=True) + z_bias
        valid = jnp.full(z.shape, past_len, jnp.int32) < past_len
        sp = jnp.where(valid, _softplus(z), 0.0)
        a = jnp.where(valid, jnp.exp(z - sp), 0.0)
        acc[...] = a.astype(BF16).astype(F32) * vn_ref[0].astype(BF16).astype(F32)
        run[...] = jnp.broadcast_to(sp, run.shape)

    flatten = lambda ref: ref[0].reshape(page * SB_HEADS, HEAD_DIM).astype(BF16)
    kp = jnp.concatenate([flatten(ref) for ref in reversed(k_refs)], axis=0)
    vp = jnp.concatenate([flatten(ref) for ref in reversed(v_refs)], axis=0)
    z = _dot_nt(qb, kp) + z_bias
    sp = jnp.where(own, _softplus(z), 0.0)
    s = sp
    step = SB_HEADS
    while step < flat:
        s = s + jnp.where(lane + step < flat, pltpu.roll(s, flat - step, axis=1), 0.0)
        step *= 2
    r = run[...]
    a = jnp.where(own, jnp.exp(z - s - r[:, :1]), 0.0)
    run[...] = r + jnp.sum(sp, axis=1, keepdims=True)
    acc[...] += _dot(a.astype(BF16), vp)

    @pl.when(p == pl.num_programs(1) - 1)
    def _():
        o = acc[...]
        y = o * lax.rsqrt(jnp.mean(o * o, axis=-1, keepdims=True) + EPS) * nw_ref[...]
        o_ref[0] = y.astype(o_ref.dtype)


def _sb_decode(proj, cache_k, cache_v, page_table, logit_bias, out_norm):
    b = proj.shape[0]
    _, page, _, _ = cache_k.shape
    n_pages = page_table.shape[1]
    pps = SB_PAGES_PER_STEP
    assert n_pages % pps == 0
    proj3 = proj.reshape(b, PROJ_MAIN // HEAD_DIM, HEAD_DIM)
    row = lambda blk: pl.BlockSpec((1, SB_HEADS, HEAD_DIM), lambda bi, p, pt: (bi, blk, 0))
    pg = lambda s: pl.BlockSpec(
        (1, page, SB_HEADS, HEAD_DIM),
        lambda bi, p, pt: (pt[bi * n_pages + n_pages - 1 - (p * pps + s)], 0, 0, 0))
    pages = [pg(s) for s in range(pps)]
    const = lambda shape: pl.BlockSpec(shape, lambda bi, p, pt: (0,) * len(shape))
    grid_spec = pltpu.PrefetchScalarGridSpec(
        num_scalar_prefetch=1,
        grid=(b, n_pages // pps),
        in_specs=[row(0), row(1), row(2), *pages, *pages,
                  const((SB_HEADS, LANES)), const((1, HEAD_DIM))],
        out_specs=pl.BlockSpec((1, SB_HEADS, HEAD_DIM), lambda bi, p, pt: (bi, 0, 0)),
        scratch_shapes=[pltpu.VMEM((SB_HEADS, HEAD_DIM), F32),
                        pltpu.VMEM((SB_HEADS, LANES), F32)],
    )
    out = pl.pallas_call(
        functools.partial(_sb_decode_kernel, past_len=n_pages * page),
        grid_spec=grid_spec,
        out_shape=jax.ShapeDtypeStruct((b, SB_HEADS, HEAD_DIM), BF16),
        compiler_params=_params("arbitrary", "arbitrary"),
        name="sb_decode",
    )(page_table.reshape(-1), proj3, proj3, proj3, *([cache_k] * pps), *([cache_v] * pps),
      jnp.broadcast_to(logit_bias[:, None], (SB_HEADS, LANES)), out_norm.reshape(1, HEAD_DIM))
    return out.reshape(b, SB_WIDTH)


def _shift_rows(x, prev, k):
    xr = pltpu.roll(x, k, axis=0)
    row = lax.broadcasted_iota(jnp.int32, prev.shape, 0)
    top = jnp.where(row < k, pltpu.roll(prev, k, axis=0), xr[:SUBLANES])
    return jnp.concatenate([top, xr[SUBLANES:]], axis=0)


def _gdn_kernel(cin_ref, z_ref, ab_ref, abt_ref, cst_ref, s0_ref, cw_ref, prow_ref, pcol_ref,
                nw_ref, o_ref, sfin_ref, s_scr, hist_scr, *, t_valid, t_total):
    c_len = GDN_C
    t = pl.program_id(1)

    @pl.when(t == 0)
    def _():
        s_scr[...] = s0_ref[0]
        hist_scr[...] = cst_ref[0]

    x = cin_ref[...]
    hist = hist_scr[...]
    cw = cw_ref[...]
    y = x * cw[3:4]
    for kk in range(1, GDN_CONV):
        y = y + _shift_rows(x, hist, kk) * cw[GDN_CONV - 1 - kk:GDN_CONV - kk]
    hist_scr[...] = x[c_len - SUBLANES:]
    conv = y * _sigmoid(y)

    ri = lax.broadcasted_iota(jnp.int32, (c_len, c_len), 0)
    ci = lax.broadcasted_iota(jnp.int32, (c_len, c_len), 1)
    tri = ri >= ci
    strict = ri > ci
    eye = (ri == ci).astype(F32)
    eye_bf = eye.astype(BF16)
    ltri = tri.astype(BF16)
    utri = (ri <= ci).astype(BF16)

    ab = ab_ref[...]
    g_col = -jnp.exp(prow_ref[0:1]) * _softplus(ab + prow_ref[1:2])
    beta_col = _sigmoid(ab)
    abt = abt_ref[0]
    g_row = (-jnp.exp(pcol_ref[0:GDN_HEADS]) *
             _softplus(abt[0:GDN_HEADS] + pcol_ref[GDN_HEADS:2 * GDN_HEADS]))
    if t_valid < t_total:
        pos_c = t * c_len + lax.broadcasted_iota(jnp.int32, (c_len, LANES), 0)
        g_col = jnp.where(pos_c < t_valid, g_col, 0.0)
        beta_col = jnp.where(pos_c < t_valid, beta_col, 0.0)
        pos_r = t * c_len + lax.broadcasted_iota(jnp.int32, (GDN_HEADS, c_len), 1)
        g_row = jnp.where(pos_r < t_valid, g_row, 0.0)
    gc_col = sum(_dot(ltri, part) for part in _split3(g_col))
    gc_row = sum(_dot(part, utri) for part in _split3(g_row))

    levels = []
    m = 1
    while m < c_len:
        lm = m.bit_length() - 1
        same = ((ri ^ ci) >> (lm + 1)) == 0
        lower = ((ri >> lm) & 1) > ((ci >> lm) & 1)
        levels.append(jnp.where(same, jnp.where(lower, 1.0, 0.0), 0.0))
        m *= 2

    heads = range(GDN_HEADS)
    head_cols = lambda base, h: slice(base + h * HEAD_DIM, base + (h + 1) * HEAD_DIM)
    qn, kn, kn_bf, kb, vb, decay, egc, gcc = [], [], [], [], [], [], [], []
    for h in heads:
        q = conv[:, head_cols(0, h)]
        k = conv[:, head_cols(GDN_WIDTH, h)]
        v = conv[:, head_cols(2 * GDN_WIDTH, h)]
        qn.append(q * lax.rsqrt(jnp.sum(q * q, axis=-1, keepdims=True) + 1e-6) * (HEAD_DIM ** -0.5))
        kn.append(k * lax.rsqrt(jnp.sum(k * k, axis=-1, keepdims=True) + 1e-6))
        kn_bf.append(kn[h].astype(BF16))
        gcc.append(gc_col[:, A_LANE + h:A_LANE + h + 1])
        beta = beta_col[:, B_LANE + h:B_LANE + h + 1]
        gcr = gc_row[h:h + 1, :]
        kb.append(kn[h] * beta)
        vb.append(v * beta)
        decay.append(jnp.where(tri, jnp.exp(jnp.where(tri, gcc[h] - gcr, 0.0)), 0.0))
        egc.append(jnp.exp(gcc[h]))
    low = [jnp.where(strict, _dot_nt(kb[h].astype(BF16), kn_bf[h]) * decay[h], 0.0) for h in heads]
    qk = [(_dot_nt(qn[h].astype(BF16), kn_bf[h]) * decay[h]).astype(BF16) for h in heads]

    xinv = [eye - low[h] * levels[0] for h in heads]
    for lvl in levels[1:]:
        xb = [xinv[h].astype(BF16) for h in heads]
        y1 = [_dot(xb[h], (low[h] * lvl).astype(BF16)).astype(BF16) for h in heads]
        xinv = [xinv[h] - _dot(y1[h], xb[h]) for h in heads]

    sol = []
    for h in heads:
        rhs = jnp.concatenate([vb[h], kb[h] * egc[h]], axis=1)
        sol.append(rhs + _dot((xinv[h] - eye).astype(BF16), rhs.astype(BF16)))
    s_old = [s_scr[h] for h in heads]
    ws_qs = [_dot(jnp.concatenate([sol[h][:, HEAD_DIM:], qn[h] * egc[h]], axis=0).astype(BF16),
                  s_old[h].astype(BF16)) for h in heads]
    v_new = [(sol[h][:, :HEAD_DIM] - ws_qs[h][:c_len]).astype(BF16) for h in heads]
    out = [ws_qs[h][c_len:] + _dot(qk[h], v_new[h]) for h in heads]
    nw = nw_ref[...]
    for h in heads:
        g_last = gcc[h][c_len - 1:c_len, :]
        kd = (kn[h] * jnp.exp(g_last - gcc[h])).astype(BF16)
        kd_t = _dot_nt(eye_bf, kd).astype(BF16)
        s_scr[h] = s_old[h] * jnp.exp(g_last) + _dot(kd_t, v_new[h])
    for h in heads:
        zg = z_ref[:, head_cols(0, h)]
        o = out[h]
        on = o * lax.rsqrt(jnp.mean(o * o, axis=-1, keepdims=True) + EPS) * nw
        o_ref[:, head_cols(0, h)] = (on * (zg * _sigmoid(zg))).astype(o_ref.dtype)

    @pl.when(t == pl.num_programs(1) - 1)
    def _():
        sfin_ref[0] = s_scr[...]


def _gdn(src, cin_blk, z_blk, ab, abt, conv_state, rec_state, conv_w, a_log, dt_bias, out_norm,
         b, t_pad, t_valid):
    c_len = GDN_C
    nt = t_pad // c_len
    cst = jnp.pad(conv_state, ((0, 0), (SUBLANES - (GDN_CONV - 1), 0), (0, 0)))
    cw = jnp.pad(conv_w, ((0, SUBLANES - GDN_CONV), (0, 0)))
    prow = jnp.zeros((SUBLANES, LANES), F32)
    prow = prow.at[0, A_LANE:B_LANE].set(a_log).at[1, A_LANE:B_LANE].set(dt_bias)
    pcol = jnp.broadcast_to(jnp.concatenate([a_log, dt_bias])[:, None], (2 * GDN_HEADS, LANES))
    abt3 = abt[A_LANE:].reshape(2 * GDN_HEADS, b * nt, c_len).transpose(1, 0, 2)
    full = lambda shape: pl.BlockSpec(shape, lambda bi, ti: (0,) * len(shape))
    return pl.pallas_call(
        functools.partial(_gdn_kernel, t_valid=t_valid, t_total=t_pad),
        grid=(b, nt),
        in_specs=[pl.BlockSpec((c_len, GDN_CONV_CH), lambda bi, ti: (bi * nt + ti, cin_blk)),
                  pl.BlockSpec((c_len, GDN_WIDTH), lambda bi, ti: (bi * nt + ti, z_blk)),
                  pl.BlockSpec((c_len, LANES), lambda bi, ti: (bi * nt + ti, 0)),
                  pl.BlockSpec((1, 2 * GDN_HEADS, c_len), lambda bi, ti: (bi * nt + ti, 0, 0)),
                  pl.BlockSpec((1, SUBLANES, GDN_CONV_CH), lambda bi, ti: (bi, 0, 0)),
                  pl.BlockSpec((1, GDN_HEADS, HEAD_DIM, HEAD_DIM), lambda bi, ti: (bi, 0, 0, 0)),
                  full((SUBLANES, GDN_CONV_CH)), full((SUBLANES, LANES)),
                  full((2 * GDN_HEADS, LANES)), full((1, HEAD_DIM))],
        out_specs=[pl.BlockSpec((c_len, GDN_WIDTH), lambda bi, ti: (bi * nt + ti, 0)),
                   pl.BlockSpec((1, GDN_HEADS, HEAD_DIM, HEAD_DIM), lambda bi, ti: (bi, 0, 0, 0))],
        out_shape=[jax.ShapeDtypeStruct((b * t_pad, GDN_WIDTH), BF16),
                   jax.ShapeDtypeStruct((b, GDN_HEADS, HEAD_DIM, HEAD_DIM), F32)],
        scratch_shapes=[pltpu.VMEM((GDN_HEADS, HEAD_DIM, HEAD_DIM), F32),
                        pltpu.VMEM((SUBLANES, GDN_CONV_CH), F32)],
        compiler_params=_params("arbitrary", "arbitrary"),
        name="gdn",
    )(src, src, ab, abt3, cst, rec_state, cw, prow, pcol, out_norm.reshape(1, HEAD_DIM))


def _rms(x, w):
    return x * lax.rsqrt(jnp.mean(x * x, axis=-1, keepdims=True) + EPS) * w


def _outproj_kernel(ms_ref, mg_ref, w0_ref, w1_ref, x_ref, nw_ref, h_ref, f_ref, w0_bf, w1_bf):
    @pl.when(pl.program_id(0) == 0)
    def _():
        w0_bf[...] = w0_ref[...].astype(BF16)
        w1_bf[...] = w1_ref[...].astype(BF16)

    for rows in _row_groups(x_ref.shape[0]):
        h = (x_ref[rows, :] + _dot(ms_ref[rows, :], w0_bf[...])
             + _dot(mg_ref[rows, :], w1_bf[...]))
        h_ref[rows, :] = h
        f_ref[rows, :] = _rms(h, nw_ref[...]).astype(f_ref.dtype)


def _outproj(mix_sb, mix_gdn, w_out, x, ffn_norm):
    m, d = x.shape
    tm = min(ROW_TILE, m)
    return pl.pallas_call(
        _outproj_kernel,
        grid=(m // tm,),
        in_specs=[pl.BlockSpec((tm, SB_WIDTH), lambda i: (i, 0)),
                  pl.BlockSpec((tm, GDN_WIDTH), lambda i: (i, 0)),
                  pl.BlockSpec((SB_WIDTH, d), lambda i: (0, 0), pipeline_mode=pl.Buffered(1)),
                  pl.BlockSpec((GDN_WIDTH, d), lambda i: (1, 0), pipeline_mode=pl.Buffered(1)),
                  pl.BlockSpec((tm, d), lambda i: (i, 0)),
                  pl.BlockSpec((1, d), lambda i: (0, 0))],
        out_specs=[pl.BlockSpec((tm, d), lambda i: (i, 0)), pl.BlockSpec((tm, d), lambda i: (i, 0))],
        out_shape=[jax.ShapeDtypeStruct((m, d), F32), jax.ShapeDtypeStruct((m, d), BF16)],
        scratch_shapes=[pltpu.VMEM((SB_WIDTH, d), BF16), pltpu.VMEM((GDN_WIDTH, d), BF16)],
        compiler_params=_params("arbitrary"),
        name="outproj",
    )(mix_sb, mix_gdn, w_out, w_out, x, ffn_norm.reshape(1, d))


def _ffn_up_kernel(f_ref, wg_ref, wu_ref, cw_ref, hist_ref, act_ref, st_ref, wg_bf, wu_bf, carry,
                   *, tiles_per_seq, single_token):
    i = pl.program_id(1)

    @pl.when(i == 0)
    def _():
        wg_bf[...] = wg_ref[...].astype(BF16)
        wu_bf[...] = wu_ref[...].astype(BF16)

    cw = cw_ref[...]
    if single_token:
        f = f_ref[...]
        g = _dot(f, wg_bf[...])
        u = _dot(f, wu_bf[...])
        h0 = hist_ref[0]
        h1 = hist_ref[1]
        gate = h0 * cw[0:1] + h1 * cw[1:2] + g * cw[2:3]
        st_ref[0] = h1
        st_ref[1] = g
        act_ref[...] = (gate * _sigmoid(gate) * u).astype(act_ref.dtype)
    else:
        @pl.when(i % tiles_per_seq == 0)
        def _():
            carry[...] = hist_ref[0]

        prev = carry[...]
        for rows in _row_groups(f_ref.shape[0]):
            sub = rows.stop - rows.start
            f = f_ref[rows, :]
            g = _dot(f, wg_bf[...])
            u = _dot(f, wu_bf[...])
            gate = (_shift_rows(g, prev, 2) * cw[0:1] + _shift_rows(g, prev, 1) * cw[1:2]
                    + g * cw[2:3])
            act_ref[rows, :] = (gate * _sigmoid(gate) * u).astype(act_ref.dtype)
            prev = g[sub - SUBLANES:]
        carry[...] = prev
        st_ref[0] = prev


def _ffn_up(f, w_gate, w_up, conv_w, state, seq_len):
    m, d = f.shape
    dff = w_gate.shape[1]
    tm = WIDE_TILE if _many_rows(m) else m
    tn = COL_TILE
    nj = pl.cdiv(dff, tn)
    cw = jnp.pad(conv_w, ((0, SUBLANES - FFN_CONV), (0, 0)))
    single = seq_len == 1
    nb = m // seq_len
    if single:
        assert tm == m
        hist = state.transpose(1, 0, 2)
        hist_spec = pl.BlockSpec((FFN_CONV - 1, m, tn), lambda j, i: (0, 0, j))
        st_spec = pl.BlockSpec((FFN_CONV - 1, m, tn), lambda j, i: (0, 0, j))
        st_shape = jax.ShapeDtypeStruct((FFN_CONV - 1, m, dff), F32)
        tiles_per_seq = 1
    else:
        tiles_per_seq = seq_len // tm
        hist = jnp.pad(state, ((0, 0), (SUBLANES - (FFN_CONV - 1), 0), (0, 0)))
        hist_spec = pl.BlockSpec((1, SUBLANES, tn), lambda j, i: (i // tiles_per_seq, 0, j))
        st_spec = pl.BlockSpec((1, SUBLANES, tn), lambda j, i: (i // tiles_per_seq, 0, j))
        st_shape = jax.ShapeDtypeStruct((nb, SUBLANES, dff), F32)
    act, st = pl.pallas_call(
        functools.partial(_ffn_up_kernel, tiles_per_seq=tiles_per_seq, single_token=single),
        grid=(nj, m // tm),
        in_specs=[pl.BlockSpec((tm, d), lambda j, i: (i, 0)),
                  pl.BlockSpec((d, tn), lambda j, i: (0, j)),
                  pl.BlockSpec((d, tn), lambda j, i: (0, j)),
                  pl.BlockSpec((SUBLANES, tn), lambda j, i: (0, j)),
                  hist_spec],
        out_specs=[pl.BlockSpec((tm, tn), lambda j, i: (i, j)), st_spec],
        out_shape=[jax.ShapeDtypeStruct((m, dff), BF16), st_shape],
        scratch_shapes=[pltpu.VMEM((d, tn), BF16), pltpu.VMEM((d, tn), BF16),
                        pltpu.VMEM((SUBLANES, tn), F32)],
        compiler_params=_params("arbitrary", "arbitrary"),
        name="ffn_up",
    )(f, w_gate, w_up, cw, hist)
    if single:
        return act, st.transpose(1, 0, 2)
    return act, st[:, SUBLANES - (FFN_CONV - 1):]


def _ffn_down_kernel(act_ref, w_ref, h_ref, o_ref, w_bf):
    @pl.when(pl.program_id(1) == 0)
    def _():
        w_bf[...] = w_ref[...].astype(BF16)

    o_ref[...] = h_ref[...] + _dot(act_ref[...], w_bf[...])


def _ffn_down(act, w_down, h):
    m, dff = act.shape
    d = h.shape[1]
    tm, tn = (ROW_TILE, WIDE_TILE) if _many_rows(m) else (m, COL_TILE)
    return pl.pallas_call(
        _ffn_down_kernel,
        grid=(d // tn, m // tm),
        in_specs=[pl.BlockSpec((tm, dff), lambda j, i: (i, 0)),
                  _weight_spec((dff, tn), lambda j, i: (0, j), m),
                  pl.BlockSpec((tm, tn), lambda j, i: (i, j))],
        out_specs=pl.BlockSpec((tm, tn), lambda j, i: (i, j)),
        out_shape=jax.ShapeDtypeStruct((m, d), F32),
        scratch_shapes=[pltpu.VMEM((dff, tn), BF16)],
        compiler_params=_params("arbitrary", "arbitrary"),
        name="ffn_down",
    )(act, w_down, h)


def _ple_kernel(p_ref, wg_ref, wp_ref, h_ref, pn_ref, nw_ref, o_ref, wg_bf, wp_bf):
    @pl.when(pl.program_id(0) == 0)
    def _():
        wg_bf[...] = wg_ref[...].astype(BF16)
        wp_bf[...] = wp_ref[...].astype(BF16)

    for rows in _row_groups(h_ref.shape[0]):
        h = h_ref[rows, :]
        gate = _sigmoid(_dot(_rms(h, pn_ref[...]).astype(BF16), wg_bf[...]))
        emb = _dot(p_ref[rows, :].astype(BF16), wp_bf[...])
        o_ref[rows, :] = _rms(h + emb * gate, nw_ref[...])


def _ple(p, w_gate, w_proj, h, ple_norm, final_norm):
    m, d = h.shape
    pd = p.shape[1]
    tm = min(ROW_TILE, m)
    return pl.pallas_call(
        _ple_kernel,
        grid=(m // tm,),
        in_specs=[pl.BlockSpec((tm, pd), lambda i: (i, 0)),
                  pl.BlockSpec((d, d), lambda i: (0, 0), pipeline_mode=pl.Buffered(1)),
                  pl.BlockSpec((pd, d), lambda i: (0, 0), pipeline_mode=pl.Buffered(1)),
                  pl.BlockSpec((tm, d), lambda i: (i, 0)),
                  pl.BlockSpec((1, d), lambda i: (0, 0)),
                  pl.BlockSpec((1, d), lambda i: (0, 0))],
        out_specs=pl.BlockSpec((tm, d), lambda i: (i, 0)),
        out_shape=jax.ShapeDtypeStruct((m, d), F32),
        scratch_shapes=[pltpu.VMEM((d, d), BF16), pltpu.VMEM((pd, d), BF16)],
        compiler_params=_params("arbitrary"),
        name="ple",
    )(p, w_gate, w_proj, h, ple_norm.reshape(1, d), final_norm.reshape(1, d))


def _layer(x, p, sb_fn, gdn_conv_state, gdn_rec_state, ffn_conv_state, weights, final_norm):
    (attn_norm, w_in, sb_out_norm, gdn_conv_w, gdn_a_log, gdn_dt_bias, gdn_out_norm, w_out,
     ffn_norm, w_ffn_gate, w_ffn_up, ffn_conv_w, w_ffn_down, ple_norm, w_ple_gate,
     w_ple_proj) = weights
    b, t, d = x.shape
    m = b * t
    tm = min(ROW_TILE, m)
    x2 = x.reshape(m, d)

    a = _rmsnorm(x2, attn_norm, BF16, tm)
    proj, ab, abt, sb_k, sb_v = _proj(a, w_in)
    mix_sb = sb_fn(proj)

    conv_in = proj.reshape(b, t, PROJ_MAIN)[:, :, O_GDN_QKV:O_GDN_Z]
    if t % GDN_C == 0:
        mix_gdn, rec = _gdn(proj, O_GDN_QKV // GDN_CONV_CH, O_GDN_Z // GDN_WIDTH, ab, abt,
                            gdn_conv_state, gdn_rec_state, gdn_conv_w, gdn_a_log, gdn_dt_bias,
                            gdn_out_norm, b, t, t)
    else:
        t_pad = -(-t // GDN_C) * GDN_C
        pad_rows = lambda arr: jnp.pad(arr.reshape(b, t, -1), ((0, 0), (0, t_pad - t), (0, 0))
                                       ).reshape(b * t_pad, -1)
        src = pad_rows(proj[:, O_GDN_QKV:])
        abt_pad = jnp.pad(abt.reshape(LANES, b, t), ((0, 0), (0, 0), (0, t_pad - t))
                          ).reshape(LANES, b * t_pad)
        mix_gdn, rec = _gdn(src, 0, GDN_CONV_CH // GDN_WIDTH, pad_rows(ab), abt_pad,
                            gdn_conv_state, gdn_rec_state, gdn_conv_w, gdn_a_log, gdn_dt_bias,
                            gdn_out_norm, b, t_pad, t)
        mix_gdn = mix_gdn.reshape(b, t_pad, GDN_WIDTH)[:, :t].reshape(m, GDN_WIDTH)
    keep = GDN_CONV - 1
    new_gdn_conv = jnp.concatenate([gdn_conv_state[:, min(t, keep):],
                                    conv_in[:, max(t - keep, 0):]], axis=1)

    h, f = _outproj(mix_sb, mix_gdn, w_out, x2, ffn_norm)
    act, new_ffn_conv = _ffn_up(f, w_ffn_gate, w_ffn_up, ffn_conv_w, ffn_conv_state, t)
    h = _ffn_down(act, w_ffn_down, h)
    y = _ple(p.reshape(m, -1), w_ple_gate, w_ple_proj, h, ple_norm, final_norm)

    sb_k = sb_k.reshape(b, t, SB_HEADS, HEAD_DIM)
    sb_v = sb_v.reshape(b, t, SB_HEADS, HEAD_DIM)
    return y.reshape(b, t, d), sb_k, sb_v, new_gdn_conv, rec, new_ffn_conv


def kernel(x_prompt, x_sample, cache_sb_k, cache_sb_v, page_table, state_gdn_conv, state_gdn_rec, state_ffn_conv, p_prompt, p_sample, attn_norm, w_in, sb_logit_bias, sb_out_norm, gdn_conv_w, gdn_a_log, gdn_dt_bias, gdn_out_norm, w_out, ffn_norm, w_ffn_gate, w_ffn_up, ffn_conv_w, w_ffn_down, ple_norm, w_ple_gate, w_ple_proj, final_norm):
    assert attn_norm.shape[0] == 1, "single-layer step"
    assert x_sample.shape[1] == 1, "decode group carries one token per sequence"
    bp, tp, _ = x_prompt.shape
    bs = x_sample.shape[0]
    weights = (attn_norm[0], w_in[0], sb_out_norm[0], gdn_conv_w[0], gdn_a_log[0], gdn_dt_bias[0],
               gdn_out_norm[0], w_out[0], ffn_norm[0], w_ffn_gate[0], w_ffn_up[0], ffn_conv_w[0],
               w_ffn_down[0], ple_norm[0], w_ple_gate[0], w_ple_proj[0])
    bias = sb_logit_bias[0]
    out_norm = sb_out_norm[0]

    zeros = lambda *shape: jnp.zeros(shape, x_prompt.dtype)
    out_p = _layer(
        x_prompt, p_prompt[0],
        lambda proj: _sb_prompt(proj, bias, out_norm, bp, tp),
        zeros(bp, GDN_CONV - 1, GDN_CONV_CH), zeros(bp, GDN_HEADS, HEAD_DIM, HEAD_DIM),
        zeros(bp, FFN_CONV - 1, w_ffn_gate.shape[2]), weights, final_norm)
    out_s = _layer(
        x_sample, p_sample[0],
        lambda proj: _sb_decode(proj, cache_sb_k[0], cache_sb_v[0], page_table, bias, out_norm),
        state_gdn_conv[0], state_gdn_rec[0], state_ffn_conv[0], weights, final_norm)

    (y_p, *st_p), (y_s, *st_s) = out_p, out_s
    return (y_p, y_s, *[s[None] for s in st_p], *[s[None] for s in st_s])
```

```python
import functools

import jax
import jax.numpy as jnp
from jax import lax
from jax.experimental import pallas as pl
from jax.experimental.pallas import tpu as pltpu

F32 = jnp.float32
BF16 = jnp.bfloat16

EPS = 1e-6
HEAD_DIM = 128
SB_HEADS = 8
GDN_HEADS = 8
SB_WIDTH = SB_HEADS * HEAD_DIM
GDN_WIDTH = GDN_HEADS * HEAD_DIM
GDN_CONV = 4
GDN_CONV_CH = 3 * GDN_WIDTH
FFN_CONV = 3
SB_SCALE = HEAD_DIM ** -0.5
LOG2_E = 1.4426950408889634
O_GDN_QKV = 3 * SB_WIDTH
O_GDN_Z = O_GDN_QKV + GDN_CONV_CH
PROJ_MAIN = O_GDN_Z + GDN_WIDTH

LANES = 128
SUBLANES = 8
VMEM_LIMIT = 56 * 1024 * 1024
COL_TILE = 512
ROW_TILE = 512
WIDE_TILE = 1024
SUB_ROWS = 256
SB_TQ = 512
SB_PAGES_PER_STEP = 8
GDN_C = 128
A_LANE = LANES - 2 * GDN_HEADS
B_LANE = LANES - GDN_HEADS

_NT = (((1,), (1,)), ((), ()))


def _params(*sem):
    return pltpu.CompilerParams(dimension_semantics=sem, vmem_limit_bytes=VMEM_LIMIT)


def _dot(a, b):
    return jnp.dot(a, b, preferred_element_type=F32)


def _dot_nt(a, b):
    return lax.dot_general(a, b, _NT, preferred_element_type=F32)


def _softplus(x):
    return jnp.maximum(x, 0.0) + jnp.log1p(jnp.exp(-jnp.abs(x)))


def _softplus2(x):
    return jnp.maximum(x, 0.0) + jnp.log2(1.0 + jnp.exp2(-jnp.abs(x)))


def _sigmoid(x):
    return 1.0 / (1.0 + jnp.exp(-x))


def _row_groups(rows):
    sub = min(SUB_ROWS, rows)
    return [slice(r, r + sub) for r in range(0, rows, sub)]


def _split2(x):
    hi = x.astype(BF16)
    lo = (x - hi.astype(F32)).astype(BF16)
    return hi, lo


def _split3(x):
    h1 = x.astype(BF16)
    r1 = x - h1.astype(F32)
    h2 = r1.astype(BF16)
    h3 = (r1 - h2.astype(F32)).astype(BF16)
    return h1, h2, h3


def _rmsnorm_kernel(x_ref, w_ref, o_ref):
    x = x_ref[...]
    y = x * lax.rsqrt(jnp.mean(x * x, axis=-1, keepdims=True) + EPS)
    o_ref[...] = (y * w_ref[...]).astype(o_ref.dtype)


def _rmsnorm(x, w, out_dtype, tm):
    m, d = x.shape
    return pl.pallas_call(
        _rmsnorm_kernel,
        grid=(m // tm,),
        in_specs=[pl.BlockSpec((tm, d), lambda i: (i, 0)),
                  pl.BlockSpec((1, d), lambda i: (0, 0))],
        out_specs=pl.BlockSpec((tm, d), lambda i: (i, 0)),
        out_shape=jax.ShapeDtypeStruct((m, d), out_dtype),
        compiler_params=_params("parallel"),
        name="rmsnorm",
    )(x, w.reshape(1, d))


def _is_last(axis):
    return pl.program_id(axis) == pl.num_programs(axis) - 1


def _proj_kernel(a_ref, as_ref, w_ref, wg_ref, wgt_ref, o_ref, ab_ref, abt_ref, k_ref, v_ref,
                 os_ref, abs_ref, abts_ref, wbf_ref):
    @pl.when(pl.program_id(1) == 0)
    def _():
        wbf_ref[...] = w_ref[...].astype(BF16)

    o_ref[...] = _dot(a_ref[...], wbf_ref[...])

    @pl.when(_is_last(1))
    def _():
        os_ref[...] = _dot(as_ref[...], wbf_ref[...])

    @pl.when(pl.program_id(0) == 0)
    def _():
        for src, dst, dst_t in ((a_ref, ab_ref, abt_ref), (as_ref, abs_ref, abts_ref)):
            a = src[...]
            dst[...] = _dot(a, wg_ref[...])
            dst_t[...] = _dot_nt(wgt_ref[...], a)

    rows = o_ref.shape[0]
    for blk, ref in ((1, k_ref), (2, v_ref)):
        @pl.when(pl.program_id(0) == blk)
        def _(ref=ref):
            for h in range(SB_HEADS):
                ref[pl.ds(h, rows, stride=SB_HEADS), :] = o_ref[:, h * HEAD_DIM:(h + 1) * HEAD_DIM]


def _proj(a, a_s, w_in):
    m, d = a.shape
    ms = a_s.shape[0]
    tm = tn = WIDE_TILE
    assert m % tm == 0 and tn == SB_WIDTH
    n_m = m // tm

    def during(blk):
        return lambda j, i: jnp.where(j == blk, i, jnp.where(j < blk, 0, n_m - 1))

    head_major = lambda blk: pl.BlockSpec((tm * SB_HEADS, HEAD_DIM),
                                          lambda j, i: (during(blk)(j, i), 0))
    w_tail = w_in[0, :, w_in.shape[2] - LANES:].astype(BF16)
    proj, ab, abt, sb_k, sb_v, proj_s, ab_s, abt_s = pl.pallas_call(
        _proj_kernel,
        grid=(PROJ_MAIN // tn, n_m),
        in_specs=[pl.BlockSpec((tm, d), lambda j, i: (i, 0)),
                  pl.BlockSpec((ms, d), lambda j, i: (0, 0)),
                  pl.BlockSpec((None, d, tn), lambda j, i: (0, 0, j), pipeline_mode=pl.Buffered(1)),
                  pl.BlockSpec((d, LANES), lambda j, i: (0, 0)),
                  pl.BlockSpec((LANES, d), lambda j, i: (0, 0))],
        out_specs=[pl.BlockSpec((tm, tn), lambda j, i: (i, j)),
                   pl.BlockSpec((tm, LANES), lambda j, i: (during(0)(j, i), 0)),
                   pl.BlockSpec((LANES, tm), lambda j, i: (0, during(0)(j, i))),
                   head_major(1), head_major(2),
                   pl.BlockSpec((ms, tn), lambda j, i: (0, j)),
                   pl.BlockSpec((ms, LANES), lambda j, i: (0, 0)),
                   pl.BlockSpec((LANES, ms), lambda j, i: (0, 0))],
        out_shape=[jax.ShapeDtypeStruct((m, PROJ_MAIN), F32),
                   jax.ShapeDtypeStruct((m, LANES), F32),
                   jax.ShapeDtypeStruct((LANES, m), F32),
                   jax.ShapeDtypeStruct((m * SB_HEADS, HEAD_DIM), F32),
                   jax.ShapeDtypeStruct((m * SB_HEADS, HEAD_DIM), F32),
                   jax.ShapeDtypeStruct((ms, PROJ_MAIN), F32),
                   jax.ShapeDtypeStruct((ms, LANES), F32),
                   jax.ShapeDtypeStruct((LANES, ms), F32)],
        scratch_shapes=[pltpu.VMEM((d, tn), BF16)],
        compiler_params=_params("arbitrary", "arbitrary"),
        name="proj",
    )(a, a_s, w_in, w_tail, w_tail.T)
    heads = lambda x, rows: x.reshape(rows, SB_HEADS, HEAD_DIM)
    return ((proj, ab, abt, heads(sb_k, m), heads(sb_v, m)),
            (proj_s, ab_s, abt_s, heads(proj_s[:, SB_WIDTH:2 * SB_WIDTH], ms),
             heads(proj_s[:, 2 * SB_WIDTH:3 * SB_WIDTH], ms)))


def _cumsum_weights():
    j = lax.broadcasted_iota(jnp.int32, (LANES, LANES), 0)
    s = lax.broadcasted_iota(jnp.int32, (LANES, LANES), 1)
    incl = (j >= s).astype(BF16)
    half = jnp.concatenate([incl, jnp.ones((LANES, LANES), BF16)], axis=1)
    return jnp.concatenate([half, half], axis=0)


def _sb_chunk(z, vc, w2, r, diag):
    n = vc.shape[0] // LANES
    blocks = []
    for c in range(n):
        row0 = c * LANES if diag else 0
        zc = z[row0:, c * LANES:(c + 1) * LANES]
        sp = _softplus2(zc)
        valid = None
        if diag:
            valid = (lax.broadcasted_iota(jnp.int32, zc.shape, 1)
                     < lax.broadcasted_iota(jnp.int32, zc.shape, 0))
            sp = jnp.where(valid, sp, 0.0)
        hi, lo = _split2(sp)
        blocks.append((row0, zc, valid, jnp.concatenate([hi, lo], axis=1)))
    res = _dot(jnp.concatenate([blk[3] for blk in blocks], axis=0), w2)
    ends = []
    for blk in blocks:
        ends.append((ends[-1] if ends else 0) + blk[1].shape[0])
    a_parts = [None] * n
    for c in reversed(range(n)):
        row0, zc, valid, _ = blocks[c]
        rc = res[ends[c] - zc.shape[0]:ends[c]]
        a = jnp.exp2(zc - rc[:, :LANES] - r[row0:])
        if diag:
            a = jnp.where(valid, a, 0.0)
        a = a.astype(BF16)
        r_new = r[row0:] + rc[:, LANES:]
        if row0:
            a = jnp.concatenate([jnp.zeros((row0, LANES), BF16), a], axis=0)
            r_new = jnp.concatenate([r[:row0], r_new], axis=0)
        a_parts[c] = a
        r = r_new
    return _dot(jnp.concatenate(a_parts, axis=1), vc), r


def _sb_prompt_kernel(c_ref, q_ref, k_ref, v_ref, w2_ref, nw_ref, o_ref, kbf, vbf, acc, run, zbuf):
    h = pl.program_id(1)
    i = pl.program_id(2)

    @pl.when(i == 0)
    def _():
        kbf[...] = k_ref[...].astype(BF16)
        vbf[...] = v_ref[...].astype(BF16)

    z_bias = c_ref[h] * LOG2_E
    w2 = w2_ref[...]
    q = (q_ref[...] * (SB_SCALE * LOG2_E)).astype(BF16)

    def rows_of(chunk):
        return pl.ds(pl.multiple_of(jnp.maximum(chunk, 0) * SB_TQ, SB_TQ), SB_TQ)

    def logits(chunk):
        return _dot_nt(q, kbf[rows_of(chunk), :]) + z_bias

    zbuf[0] = logits(i - 1)
    d, r = _sb_chunk(logits(i), vbf[rows_of(i), :], w2, jnp.zeros(run.shape, F32), True)
    acc[...] = d
    run[...] = r

    def step(chunk, slot, more):
        if more:
            zbuf[1 - slot] = logits(chunk - 1)
        d, r = _sb_chunk(zbuf[slot], vbf[rows_of(chunk), :], w2, run[...], False)
        acc[...] += d
        run[...] = r

    def body(j, carry):
        step(i - 1 - 2 * j, 0, True)
        step(i - 2 - 2 * j, 1, True)
        return carry

    lax.fori_loop(0, i // 2, body, 0)

    @pl.when(i % 2 == 1)
    def _():
        step(0, 0, False)

    o = acc[...]
    y = o * lax.rsqrt(jnp.mean(o * o, axis=-1, keepdims=True) + EPS) * nw_ref[...]
    o_ref[...] = y.astype(o_ref.dtype)


def _sb_prompt(proj, logit_bias, out_norm, b, t):
    nq = t // SB_TQ
    blk = lambda rows, fn: pl.BlockSpec((rows, HEAD_DIM), fn)
    return pl.pallas_call(
        _sb_prompt_kernel,
        grid=(b, SB_HEADS, nq),
        in_specs=[pl.BlockSpec(memory_space=pltpu.SMEM),
                  blk(SB_TQ, lambda bi, h, i: (bi * nq + i, h)),
                  blk(t, lambda bi, h, i: (bi, SB_HEADS + h)),
                  blk(t, lambda bi, h, i: (bi, 2 * SB_HEADS + h)),
                  pl.BlockSpec((2 * LANES, 2 * LANES), lambda bi, h, i: (0, 0)),
                  pl.BlockSpec((1, HEAD_DIM), lambda bi, h, i: (0, 0))],
        out_specs=blk(SB_TQ, lambda bi, h, i: (bi * nq + i, h)),
        out_shape=jax.ShapeDtypeStruct((b * t, SB_WIDTH), BF16),
        scratch_shapes=[pltpu.VMEM((t, HEAD_DIM), BF16), pltpu.VMEM((t, HEAD_DIM), BF16),
                        pltpu.VMEM((SB_TQ, HEAD_DIM), F32), pltpu.VMEM((SB_TQ, LANES), F32),
                        pltpu.VMEM((2, SB_TQ, SB_TQ), F32)],
        compiler_params=_params("arbitrary", "arbitrary", "arbitrary"),
        name="sb_prompt",
    )(logit_bias, proj, proj, proj, _cumsum_weights(), out_norm.reshape(1, HEAD_DIM))


def _sb_decode_kernel(pt_ref, q_ref, kn_ref, vn_ref, *rest, past_len):
    k_refs = rest[:SB_PAGES_PER_STEP]
    v_refs = rest[SB_PAGES_PER_STEP:2 * SB_PAGES_PER_STEP]
    c_ref, nw_ref, o_ref, acc, run = rest[2 * SB_PAGES_PER_STEP:]
    p = pl.program_id(1)
    page = k_refs[0].shape[1]
    flat = SB_PAGES_PER_STEP * page * SB_HEADS
    lane = lax.broadcasted_iota(jnp.int32, (SB_HEADS, flat), 1)
    own = (lane % SB_HEADS) == lax.broadcasted_iota(jnp.int32, (SB_HEADS, flat), 0)
    z_bias = c_ref[...][:, :1]
    qb = (q_ref[0] * SB_SCALE).astype(BF16)

    @pl.when(p == 0)
    def _():
        kn = kn_ref[0].astype(BF16).astype(F32)
        z = jnp.sum(qb.astype(F32) * kn, axis=-1, keepdims=True) + z_bias
        valid = jnp.full(z.shape, past_len, jnp.int32) < past_len
        sp = jnp.where(valid, _softplus(z), 0.0)
        a = jnp.where(valid, jnp.exp(z - sp), 0.0)
        acc[...] = a.astype(BF16).astype(F32) * vn_ref[0].astype(BF16).astype(F32)
        run[...] = jnp.broadcast_to(sp, run.shape)

    flatten = lambda ref: ref[0].reshape(page * SB_HEADS, HEAD_DIM).astype(BF16)
    kp = jnp.concatenate([flatten(ref) for ref in reversed(k_refs)], axis=0)
    vp = jnp.concatenate([flatten(ref) for ref in reversed(v_refs)], axis=0)
    z = _dot_nt(qb, kp) + z_bias
    sp = jnp.where(own, _softplus(z), 0.0)
    s = sp
    step = SB_HEADS
    while step < flat:
        s = s + jnp.where(lane + step < flat, pltpu.roll(s, flat - step, axis=1), 0.0)
        step *= 2
    r = run[...]
    a = jnp.where(own, jnp.exp(z - s - r[:, :1]), 0.0)
    run[...] = r + jnp.sum(sp, axis=1, keepdims=True)
    acc[...] += _dot(a.astype(BF16), vp)

    @pl.when(p == pl.num_programs(1) - 1)
    def _():
        o = acc[...]
        y = o * lax.rsqrt(jnp.mean(o * o, axis=-1, keepdims=True) + EPS) * nw_ref[...]
        o_ref[0] = y.astype(o_ref.dtype)


def _sb_decode(proj, cache_k, cache_v, page_table, logit_bias, out_norm):
    b = proj.shape[0]
    _, page, _, _ = cache_k.shape
    n_pages = page_table.shape[1]
    pps = SB_PAGES_PER_STEP
    assert n_pages % pps == 0
    proj3 = proj.reshape(b, PROJ_MAIN // HEAD_DIM, HEAD_DIM)
    row = lambda blk: pl.BlockSpec((1, SB_HEADS, HEAD_DIM), lambda bi, p, pt: (bi, blk, 0))
    pg = lambda s: pl.BlockSpec(
        (1, page, SB_HEADS, HEAD_DIM),
        lambda bi, p, pt: (pt[bi * n_pages + n_pages - 1 - (p * pps + s)], 0, 0, 0))
    pages = [pg(s) for s in range(pps)]
    const = lambda shape: pl.BlockSpec(shape, lambda bi, p, pt: (0,) * len(shape))
    grid_spec = pltpu.PrefetchScalarGridSpec(
        num_scalar_prefetch=1,
        grid=(b, n_pages // pps),
        in_specs=[row(0), row(1), row(2), *pages, *pages,
                  const((SB_HEADS, LANES)), const((1, HEAD_DIM))],
        out_specs=pl.BlockSpec((1, SB_HEADS, HEAD_DIM), lambda bi, p, pt: (bi, 0, 0)),
        scratch_shapes=[pltpu.VMEM((SB_HEADS, HEAD_DIM), F32),
                        pltpu.VMEM((SB_HEADS, LANES), F32)],
    )
    out = pl.pallas_call(
        functools.partial(_sb_decode_kernel, past_len=n_pages * page),
        grid_spec=grid_spec,
        out_shape=jax.ShapeDtypeStruct((b, SB_HEADS, HEAD_DIM), BF16),
        compiler_params=_params("arbitrary", "arbitrary"),
        name="sb_decode",
    )(page_table.reshape(-1), proj3, proj3, proj3, *([cache_k] * pps), *([cache_v] * pps),
      jnp.broadcast_to(logit_bias[:, None], (SB_HEADS, LANES)), out_norm.reshape(1, HEAD_DIM))
    return out.reshape(b, SB_WIDTH)


def _shift_rows(x, prev, k):
    xr = pltpu.roll(x, k, axis=0)
    row = lax.broadcasted_iota(jnp.int32, prev.shape, 0)
    top = jnp.where(row < k, pltpu.roll(prev, k, axis=0), xr[:SUBLANES])
    return jnp.concatenate([top, xr[SUBLANES:]], axis=0)


def _gdn_kernel(cin_ref, z_ref, ab_ref, abt_ref, cst_ref, s0_ref, cw_ref, prow_ref, pcol_ref,
                nw_ref, o_ref, sfin_ref, s_scr, hist_scr, *, t_valid, t_total):
    c_len = GDN_C
    t = pl.program_id(1)

    @pl.when(t == 0)
    def _():
        s_scr[...] = s0_ref[0]
        hist_scr[...] = cst_ref[0]

    x = cin_ref[...]
    hist = hist_scr[...]
    cw = cw_ref[...]
    y = x * cw[3:4]
    for kk in range(1, GDN_CONV):
        y = y + _shift_rows(x, hist, kk) * cw[GDN_CONV - 1 - kk:GDN_CONV - kk]
    hist_scr[...] = x[c_len - SUBLANES:]
    conv = y * _sigmoid(y)

    ri = lax.broadcasted_iota(jnp.int32, (c_len, c_len), 0)
    ci = lax.broadcasted_iota(jnp.int32, (c_len, c_len), 1)
    tri = ri >= ci
    strict = ri > ci
    eye = (ri == ci).astype(F32)
    eye_bf = eye.astype(BF16)
    ltri = tri.astype(BF16)
    utri = (ri <= ci).astype(BF16)

    ab = ab_ref[...]
    g_col = -jnp.exp(prow_ref[0:1]) * _softplus(ab + prow_ref[1:2])
    beta_col = _sigmoid(ab)
    abt = abt_ref[0]
    g_row = (-jnp.exp(pcol_ref[0:GDN_HEADS]) *
             _softplus(abt[0:GDN_HEADS] + pcol_ref[GDN_HEADS:2 * GDN_HEADS]))
    if t_valid < t_total:
        pos_c = t * c_len + lax.broadcasted_iota(jnp.int32, (c_len, LANES), 0)
        g_col = jnp.where(pos_c < t_valid, g_col, 0.0)
        beta_col = jnp.where(pos_c < t_valid, beta_col, 0.0)
        pos_r = t * c_len + lax.broadcasted_iota(jnp.int32, (GDN_HEADS, c_len), 1)
        g_row = jnp.where(pos_r < t_valid, g_row, 0.0)
    gc_col = sum(_dot(ltri, part) for part in _split3(g_col))
    gc_row = sum(_dot(part, utri) for part in _split3(g_row))

    levels = []
    m = 1
    while m < c_len:
        lm = m.bit_length() - 1
        same = ((ri ^ ci) >> (lm + 1)) == 0
        lower = ((ri >> lm) & 1) > ((ci >> lm) & 1)
        levels.append(jnp.where(same, jnp.where(lower, 1.0, 0.0), 0.0))
        m *= 2

    heads = range(GDN_HEADS)
    head_cols = lambda base, h: slice(base + h * HEAD_DIM, base + (h + 1) * HEAD_DIM)
    qn, kn, kn_bf, kb, vb, decay, egc, gcc = [], [], [], [], [], [], [], []
    for h in heads:
        q = conv[:, head_cols(0, h)]
        k = conv[:, head_cols(GDN_WIDTH, h)]
        v = conv[:, head_cols(2 * GDN_WIDTH, h)]
        qn.append(q * lax.rsqrt(jnp.sum(q * q, axis=-1, keepdims=True) + 1e-6) * (HEAD_DIM ** -0.5))
        kn.append(k * lax.rsqrt(jnp.sum(k * k, axis=-1, keepdims=True) + 1e-6))
        kn_bf.append(kn[h].astype(BF16))
        gcc.append(gc_col[:, A_LANE + h:A_LANE + h + 1])
        beta = beta_col[:, B_LANE + h:B_LANE + h + 1]
        gcr = gc_row[h:h + 1, :]
        kb.append(kn[h] * beta)
        vb.append(v * beta)
        decay.append(jnp.where(tri, jnp.exp(jnp.where(tri, gcc[h] - gcr, 0.0)), 0.0))
        egc.append(jnp.exp(gcc[h]))
    low = [jnp.where(strict, _dot_nt(kb[h].astype(BF16), kn_bf[h]) * decay[h], 0.0) for h in heads]
    qk = [(_dot_nt(qn[h].astype(BF16), kn_bf[h]) * decay[h]).astype(BF16) for h in heads]

    xinv = [eye - low[h] * levels[0] for h in heads]
    for lvl in levels[1:]:
        xb = [xinv[h].astype(BF16) for h in heads]
        y1 = [_dot(xb[h], (low[h] * lvl).astype(BF16)).astype(BF16) for h in heads]
        xinv = [xinv[h] - _dot(y1[h], xb[h]) for h in heads]

    sol = []
    for h in heads:
        rhs = jnp.concatenate([vb[h], kb[h] * egc[h]], axis=1)
        sol.append(rhs + _dot((xinv[h] - eye).astype(BF16), rhs.astype(BF16)))
    s_old = [s_scr[h] for h in heads]
    ws_qs = [_dot(jnp.concatenate([sol[h][:, HEAD_DIM:], qn[h] * egc[h]], axis=0).astype(BF16),
                  s_old[h].astype(BF16)) for h in heads]
    v_new = [(sol[h][:, :HEAD_DIM] - ws_qs[h][:c_len]).astype(BF16) for h in heads]
    out = [ws_qs[h][c_len:] + _dot(qk[h], v_new[h]) for h in heads]
    nw = nw_ref[...]
    for h in heads:
        g_last = gcc[h][c_len - 1:c_len, :]
        kd = (kn[h] * jnp.exp(g_last - gcc[h])).astype(BF16)
        kd_t = _dot_nt(eye_bf, kd).astype(BF16)
        s_scr[h] = s_old[h] * jnp.exp(g_last) + _dot(kd_t, v_new[h])
    for h in heads:
        zg = z_ref[:, head_cols(0, h)]
        o = out[h]
        on = o * lax.rsqrt(jnp.mean(o * o, axis=-1, keepdims=True) + EPS) * nw
        o_ref[:, head_cols(0, h)] = (on * (zg * _sigmoid(zg))).astype(o_ref.dtype)

    @pl.when(t == pl.num_programs(1) - 1)
    def _():
        sfin_ref[0] = s_scr[...]


def _gdn(src, cin_blk, z_blk, ab, abt, conv_state, rec_state, conv_w, a_log, dt_bias, out_norm,
         b, t_pad, t_valid):
    c_len = GDN_C
    nt = t_pad // c_len
    cst = jnp.pad(conv_state, ((0, 0), (SUBLANES - (GDN_CONV - 1), 0), (0, 0)))
    cw = jnp.pad(conv_w, ((0, SUBLANES - GDN_CONV), (0, 0)))
    prow = jnp.zeros((SUBLANES, LANES), F32)
    prow = prow.at[0, A_LANE:B_LANE].set(a_log).at[1, A_LANE:B_LANE].set(dt_bias)
    pcol = jnp.broadcast_to(jnp.concatenate([a_log, dt_bias])[:, None], (2 * GDN_HEADS, LANES))
    abt3 = abt[A_LANE:].reshape(2 * GDN_HEADS, b * nt, c_len).transpose(1, 0, 2)
    full = lambda shape: pl.BlockSpec(shape, lambda bi, ti: (0,) * len(shape))
    return pl.pallas_call(
        functools.partial(_gdn_kernel, t_valid=t_valid, t_total=t_pad),
        grid=(b, nt),
        in_specs=[pl.BlockSpec((c_len, GDN_CONV_CH), lambda bi, ti: (bi * nt + ti, cin_blk)),
                  pl.BlockSpec((c_len, GDN_WIDTH), lambda bi, ti: (bi * nt + ti, z_blk)),
                  pl.BlockSpec((c_len, LANES), lambda bi, ti: (bi * nt + ti, 0)),
                  pl.BlockSpec((1, 2 * GDN_HEADS, c_len), lambda bi, ti: (bi * nt + ti, 0, 0)),
                  pl.BlockSpec((1, SUBLANES, GDN_CONV_CH), lambda bi, ti: (bi, 0, 0)),
                  pl.BlockSpec((1, GDN_HEADS, HEAD_DIM, HEAD_DIM), lambda bi, ti: (bi, 0, 0, 0)),
                  full((SUBLANES, GDN_CONV_CH)), full((SUBLANES, LANES)),
                  full((2 * GDN_HEADS, LANES)), full((1, HEAD_DIM))],
        out_specs=[pl.BlockSpec((c_len, GDN_WIDTH), lambda bi, ti: (bi * nt + ti, 0)),
                   pl.BlockSpec((1, GDN_HEADS, HEAD_DIM, HEAD_DIM), lambda bi, ti: (bi, 0, 0, 0))],
        out_shape=[jax.ShapeDtypeStruct((b * t_pad, GDN_WIDTH), BF16),
                   jax.ShapeDtypeStruct((b, GDN_HEADS, HEAD_DIM, HEAD_DIM), F32)],
        scratch_shapes=[pltpu.VMEM((GDN_HEADS, HEAD_DIM, HEAD_DIM), F32),
                        pltpu.VMEM((SUBLANES, GDN_CONV_CH), F32)],
        compiler_params=_params("arbitrary", "arbitrary"),
        name="gdn",
    )(src, src, ab, abt3, cst, rec_state, cw, prow, pcol, out_norm.reshape(1, HEAD_DIM))


def _rms(x, w):
    return x * lax.rsqrt(jnp.mean(x * x, axis=-1, keepdims=True) + EPS) * w


def _outproj_kernel(ms_ref, mg_ref, x_ref, ms_s, mg_s, x_s, w0_ref, w1_ref, nw_ref,
                    h_ref, f_ref, h_s, f_s, w0_bf, w1_bf):
    @pl.when(pl.program_id(0) == 0)
    def _():
        w0_bf[...] = w0_ref[...].astype(BF16)
        w1_bf[...] = w1_ref[...].astype(BF16)

    def rows_out(ms, mg, x, h_out, f_out, rows):
        h = x[rows, :] + _dot(ms[rows, :], w0_bf[...]) + _dot(mg[rows, :], w1_bf[...])
        h_out[rows, :] = h
        f_out[rows, :] = _rms(h, nw_ref[...]).astype(f_out.dtype)

    for rows in _row_groups(x_ref.shape[0]):
        rows_out(ms_ref, mg_ref, x_ref, h_ref, f_ref, rows)

    @pl.when(_is_last(0))
    def _():
        rows_out(ms_s, mg_s, x_s, h_s, f_s, slice(None))


def _outproj(mix, mix_s, w_out, x, x_s, ffn_norm):
    m, d = x.shape
    ms = x_s.shape[0]
    tm = ROW_TILE
    row = lambda width: pl.BlockSpec((tm, width), lambda i: (i, 0))
    small = lambda width: pl.BlockSpec((ms, width), lambda i: (0, 0))
    h, f, h_s, f_s = pl.pallas_call(
        _outproj_kernel,
        grid=(m // tm,),
        in_specs=[row(SB_WIDTH), row(GDN_WIDTH), row(d),
                  small(SB_WIDTH), small(GDN_WIDTH), small(d),
                  pl.BlockSpec((SB_WIDTH, d), lambda i: (0, 0), pipeline_mode=pl.Buffered(1)),
                  pl.BlockSpec((GDN_WIDTH, d), lambda i: (1, 0), pipeline_mode=pl.Buffered(1)),
                  pl.BlockSpec((1, d), lambda i: (0, 0))],
        out_specs=[row(d), row(d), small(d), small(d)],
        out_shape=[jax.ShapeDtypeStruct((m, d), F32), jax.ShapeDtypeStruct((m, d), BF16),
                   jax.ShapeDtypeStruct((ms, d), F32), jax.ShapeDtypeStruct((ms, d), BF16)],
        scratch_shapes=[pltpu.VMEM((SB_WIDTH, d), BF16), pltpu.VMEM((GDN_WIDTH, d), BF16)],
        compiler_params=_params("arbitrary"),
        name="outproj",
    )(*mix, x, *mix_s, x_s, w_out, w_out, ffn_norm.reshape(1, d))
    return (h, f), (h_s, f_s)


def _ffn_up_kernel(f_ref, f_s, wg_ref, wu_ref, cw_ref, hist_ref, hist_s, act_ref, st_ref, act_s, st_s,
                   wg_bf, wu_bf, carry, *, tiles_per_seq):
    i = pl.program_id(1)

    @pl.when(i == 0)
    def _():
        wg_bf[...] = wg_ref[...].astype(BF16)
        wu_bf[...] = wu_ref[...].astype(BF16)

    cw = cw_ref[...]

    @pl.when(i % tiles_per_seq == 0)
    def _():
        carry[...] = hist_ref[0]

    prev = carry[...]
    for rows in _row_groups(f_ref.shape[0]):
        sub = rows.stop - rows.start
        f = f_ref[rows, :]
        g = _dot(f, wg_bf[...])
        u = _dot(f, wu_bf[...])
        gate = (_shift_rows(g, prev, 2) * cw[0:1] + _shift_rows(g, prev, 1) * cw[1:2]
                + g * cw[2:3])
        act_ref[rows, :] = (gate * _sigmoid(gate) * u).astype(act_ref.dtype)
        prev = g[sub - SUBLANES:]
    carry[...] = prev
    st_ref[0] = prev

    @pl.when(_is_last(1))
    def _():
        f = f_s[...]
        g = _dot(f, wg_bf[...])
        u = _dot(f, wu_bf[...])
        h0 = hist_s[0]
        h1 = hist_s[1]
        gate = h0 * cw[0:1] + h1 * cw[1:2] + g * cw[2:3]
        st_s[0] = h1
        st_s[1] = g
        act_s[...] = (gate * _sigmoid(gate) * u).astype(act_s.dtype)


def _ffn_up(f, f_s, w_gate, w_up, conv_w, state, state_s, seq_len):
    m, d = f.shape
    ms = f_s.shape[0]
    dff = w_gate.shape[1]
    tm = WIDE_TILE
    tn = COL_TILE
    keep = FFN_CONV - 1
    tiles_per_seq = seq_len // tm
    assert seq_len % tm == 0
    cw = jnp.pad(conv_w, ((0, SUBLANES - FFN_CONV), (0, 0)))
    hist = jnp.pad(state, ((0, 0), (SUBLANES - keep, 0), (0, 0)))
    seq_blk = pl.BlockSpec((1, SUBLANES, tn), lambda j, i: (i // tiles_per_seq, 0, j))
    hist_s_blk = pl.BlockSpec((keep, ms, tn), lambda j, i: (0, 0, j))
    act, st, act_s, st_s = pl.pallas_call(
        functools.partial(_ffn_up_kernel, tiles_per_seq=tiles_per_seq),
        grid=(pl.cdiv(dff, tn), m // tm),
        in_specs=[pl.BlockSpec((tm, d), lambda j, i: (i, 0)),
                  pl.BlockSpec((ms, d), lambda j, i: (0, 0)),
                  pl.BlockSpec((d, tn), lambda j, i: (0, j)),
                  pl.BlockSpec((d, tn), lambda j, i: (0, j)),
                  pl.BlockSpec((SUBLANES, tn), lambda j, i: (0, j)),
                  seq_blk, hist_s_blk],
        out_specs=[pl.BlockSpec((tm, tn), lambda j, i: (i, j)), seq_blk,
                   pl.BlockSpec((ms, tn), lambda j, i: (0, j)), hist_s_blk],
        out_shape=[jax.ShapeDtypeStruct((m, dff), BF16),
                   jax.ShapeDtypeStruct((m // seq_len, SUBLANES, dff), F32),
                   jax.ShapeDtypeStruct((ms, dff), BF16),
                   jax.ShapeDtypeStruct((keep, ms, dff), F32)],
        scratch_shapes=[pltpu.VMEM((d, tn), BF16), pltpu.VMEM((d, tn), BF16),
                        pltpu.VMEM((SUBLANES, tn), F32)],
        compiler_params=_params("arbitrary", "arbitrary"),
        name="ffn_up",
    )(f, f_s, w_gate, w_up, cw, hist, state_s.transpose(1, 0, 2))
    return (act, st[:, SUBLANES - keep:]), (act_s, st_s.transpose(1, 0, 2))


def _ffn_down_kernel(act_ref, act_s, w_ref, h_ref, h_s, o_ref, o_s, w_bf):
    @pl.when(pl.program_id(1) == 0)
    def _():
        w_bf[...] = w_ref[...].astype(BF16)

    o_ref[...] = h_ref[...] + _dot(act_ref[...], w_bf[...])

    @pl.when(_is_last(1))
    def _():
        o_s[...] = h_s[...] + _dot(act_s[...], w_bf[...])


def _ffn_down(act, act_s, w_down, h, h_s):
    m, dff = act.shape
    ms = act_s.shape[0]
    d = h.shape[1]
    tm, tn = ROW_TILE, WIDE_TILE
    return pl.pallas_call(
        _ffn_down_kernel,
        grid=(d // tn, m // tm),
        in_specs=[pl.BlockSpec((tm, dff), lambda j, i: (i, 0)),
                  pl.BlockSpec((ms, dff), lambda j, i: (0, 0)),
                  pl.BlockSpec((dff, tn), lambda j, i: (0, j), pipeline_mode=pl.Buffered(1)),
                  pl.BlockSpec((tm, tn), lambda j, i: (i, j)),
                  pl.BlockSpec((ms, tn), lambda j, i: (0, j))],
        out_specs=[pl.BlockSpec((tm, tn), lambda j, i: (i, j)),
                   pl.BlockSpec((ms, tn), lambda j, i: (0, j))],
        out_shape=[jax.ShapeDtypeStruct((m, d), F32), jax.ShapeDtypeStruct((ms, d), F32)],
        scratch_shapes=[pltpu.VMEM((dff, tn), BF16)],
        compiler_params=_params("arbitrary", "arbitrary"),
        name="ffn_down",
    )(act, act_s, w_down, h, h_s)


def _ple_kernel(p_ref, h_ref, p_s, h_s, wg_ref, wp_ref, pn_ref, nw_ref, o_ref, o_s, wg_bf, wp_bf):
    @pl.when(pl.program_id(0) == 0)
    def _():
        wg_bf[...] = wg_ref[...].astype(BF16)
        wp_bf[...] = wp_ref[...].astype(BF16)

    def rows_out(p, h_in, out, rows):
        h = h_in[rows, :]
        gate = _sigmoid(_dot(_rms(h, pn_ref[...]).astype(BF16), wg_bf[...]))
        emb = _dot(p[rows, :].astype(BF16), wp_bf[...])
        out[rows, :] = _rms(h + emb * gate, nw_ref[...])

    for rows in _row_groups(h_ref.shape[0]):
        rows_out(p_ref, h_ref, o_ref, rows)

    @pl.when(_is_last(0))
    def _():
        rows_out(p_s, h_s, o_s, slice(None))


def _ple(p, p_s, w_gate, w_proj, h, h_s, ple_norm, final_norm):
    m, d = h.shape
    ms = h_s.shape[0]
    pd = p.shape[1]
    tm = ROW_TILE
    row = lambda width: pl.BlockSpec((tm, width), lambda i: (i, 0))
    small = lambda width: pl.BlockSpec((ms, width), lambda i: (0, 0))
    return pl.pallas_call(
        _ple_kernel,
        grid=(m // tm,),
        in_specs=[row(pd), row(d), small(pd), small(d),
                  pl.BlockSpec((d, d), lambda i: (0, 0), pipeline_mode=pl.Buffered(1)),
                  pl.BlockSpec((pd, d), lambda i: (0, 0), pipeline_mode=pl.Buffered(1)),
                  pl.BlockSpec((1, d), lambda i: (0, 0)),
                  pl.BlockSpec((1, d), lambda i: (0, 0))],
        out_specs=[row(d), small(d)],
        out_shape=[jax.ShapeDtypeStruct((m, d), F32), jax.ShapeDtypeStruct((ms, d), F32)],
        scratch_shapes=[pltpu.VMEM((d, d), BF16), pltpu.VMEM((pd, d), BF16)],
        compiler_params=_params("arbitrary"),
        name="ple",
    )(p, h, p_s, h_s, w_gate, w_proj, ple_norm.reshape(1, d), final_norm.reshape(1, d))


def _gdn_group(proj, ab, abt, conv_state, rec_state, gdn_w, b, t):
    conv_w, a_log, dt_bias, out_norm = gdn_w
    if t % GDN_C == 0:
        mix, rec = _gdn(proj, O_GDN_QKV // GDN_CONV_CH, O_GDN_Z // GDN_WIDTH, ab, abt,
                        conv_state, rec_state, conv_w, a_log, dt_bias, out_norm, b, t, t)
    else:
        t_pad = -(-t // GDN_C) * GDN_C
        pad_rows = lambda arr: jnp.pad(arr.reshape(b, t, -1), ((0, 0), (0, t_pad - t), (0, 0))
                                       ).reshape(b * t_pad, -1)
        abt_pad = jnp.pad(abt.reshape(LANES, b, t), ((0, 0), (0, 0), (0, t_pad - t))
                          ).reshape(LANES, b * t_pad)
        mix, rec = _gdn(pad_rows(proj[:, O_GDN_QKV:]), 0, GDN_CONV_CH // GDN_WIDTH, pad_rows(ab),
                        abt_pad, conv_state, rec_state, conv_w, a_log, dt_bias, out_norm,
                        b, t_pad, t)
        mix = mix.reshape(b, t_pad, GDN_WIDTH)[:, :t].reshape(b * t, GDN_WIDTH)
    keep = GDN_CONV - 1
    conv_in = proj.reshape(b, t, PROJ_MAIN)[:, :, O_GDN_QKV:O_GDN_Z]
    new_conv = jnp.concatenate([conv_state[:, min(t, keep):], conv_in[:, max(t - keep, 0):]],
                               axis=1)
    return mix, rec, new_conv


def kernel(x_prompt, x_sample, cache_sb_k, cache_sb_v, page_table, state_gdn_conv, state_gdn_rec, state_ffn_conv, p_prompt, p_sample, attn_norm, w_in, sb_logit_bias, sb_out_norm, gdn_conv_w, gdn_a_log, gdn_dt_bias, gdn_out_norm, w_out, ffn_norm, w_ffn_gate, w_ffn_up, ffn_conv_w, w_ffn_down, ple_norm, w_ple_gate, w_ple_proj, final_norm):
    assert attn_norm.shape[0] == 1, "single-layer step"
    assert x_sample.shape[1] == 1, "decode group carries one token per sequence"
    bp, tp, d = x_prompt.shape
    bs, ts, _ = x_sample.shape
    mp, ms = bp * tp, bs * ts
    dff = w_ffn_gate.shape[2]
    bias = sb_logit_bias[0]
    out_norm = sb_out_norm[0]
    gdn_w = (gdn_conv_w[0], gdn_a_log[0], gdn_dt_bias[0], gdn_out_norm[0])
    zeros = lambda *shape: jnp.zeros(shape, x_prompt.dtype)
    xp = x_prompt.reshape(mp, d)
    xs = x_sample.reshape(ms, d)

    a_p = _rmsnorm(xp, attn_norm[0], BF16, ROW_TILE)
    a_s = _rmsnorm(xs, attn_norm[0], BF16, ms)
    (proj_p, ab_p, abt_p, k_p, v_p), (proj_s, ab_s, abt_s, k_s, v_s) = _proj(a_p, a_s, w_in)

    sb_p = _sb_prompt(proj_p, bias, out_norm, bp, tp)
    gdn_p, rec_p, conv_p = _gdn_group(
        proj_p, ab_p, abt_p, zeros(bp, GDN_CONV - 1, GDN_CONV_CH),
        zeros(bp, GDN_HEADS, HEAD_DIM, HEAD_DIM), gdn_w, bp, tp)
    sb_s = _sb_decode(proj_s, cache_sb_k[0], cache_sb_v[0], page_table, bias, out_norm)
    gdn_s, rec_s, conv_s = _gdn_group(
        proj_s, ab_s, abt_s, state_gdn_conv[0], state_gdn_rec[0], gdn_w, bs, ts)

    (h_p, f_p), (h_s, f_s) = _outproj((sb_p, gdn_p), (sb_s, gdn_s), w_out[0], xp, xs, ffn_norm[0])
    (act_p, ffn_p), (act_s, ffn_s) = _ffn_up(
        f_p, f_s, w_ffn_gate[0], w_ffn_up[0], ffn_conv_w[0], zeros(bp, FFN_CONV - 1, dff),
        state_ffn_conv[0], tp)
    h_p, h_s = _ffn_down(act_p, act_s, w_ffn_down[0], h_p, h_s)
    y_p, y_s = _ple(p_prompt[0].reshape(mp, -1), p_sample[0].reshape(ms, -1), w_ple_gate[0],
                    w_ple_proj[0], h_p, h_s, ple_norm[0], final_norm)

    per_head = lambda x, b, t: x.reshape(1, b, t, SB_HEADS, HEAD_DIM)
    return (y_p.reshape(bp, tp, d), y_s.reshape(bs, ts, d),
            per_head(k_p, bp, tp), per_head(v_p, bp, tp), conv_p[None], rec_p[None], ffn_p[None],
            per_head(k_s, bs, ts), per_head(v_s, bs, ts), conv_s[None], rec_s[None], ffn_s[None])
```

```python
import functools

import jax
import jax.numpy as jnp
from jax import lax
from jax.experimental import pallas as pl
from jax.experimental.pallas import tpu as pltpu

F32 = jnp.float32
BF16 = jnp.bfloat16

EPS = 1e-6
HEAD_DIM = 128
SB_HEADS = 8
GDN_HEADS = 8
SB_WIDTH = SB_HEADS * HEAD_DIM
GDN_WIDTH = GDN_HEADS * HEAD_DIM
GDN_CONV = 4
GDN_CONV_CH = 3 * GDN_WIDTH
FFN_CONV = 3
SB_SCALE = HEAD_DIM ** -0.5
LOG2_E = 1.4426950408889634
O_GDN_QKV = 3 * SB_WIDTH
O_GDN_Z = O_GDN_QKV + GDN_CONV_CH
PROJ_MAIN = O_GDN_Z + GDN_WIDTH

LANES = 128
SUBLANES = 8
VMEM_LIMIT = 56 * 1024 * 1024
COL_TILE = 512
ROW_TILE = 512
WIDE_TILE = 1024
SUB_ROWS = 256
SB_TQ = 512
SB_PAGES_PER_STEP = 8
GDN_C = 128
GDN_C_SHORT = 16
A_LANE = LANES - 2 * GDN_HEADS
B_LANE = LANES - GDN_HEADS

_NT = (((1,), (1,)), ((), ()))


def _params(*sem):
    return pltpu.CompilerParams(dimension_semantics=sem, vmem_limit_bytes=VMEM_LIMIT)


def _dot(a, b):
    return jnp.dot(a, b, preferred_element_type=F32)


def _dot_nt(a, b):
    return lax.dot_general(a, b, _NT, preferred_element_type=F32)


def _softplus(x):
    return jnp.maximum(x, 0.0) + jnp.log1p(jnp.exp(-jnp.abs(x)))


def _softplus2(x):
    return jnp.maximum(x, 0.0) + jnp.log2(1.0 + jnp.exp2(-jnp.abs(x)))


def _sigmoid(x):
    return 1.0 / (1.0 + jnp.exp(-x))


def _row_groups(rows):
    sub = min(SUB_ROWS, rows)
    return [slice(r, r + sub) for r in range(0, rows, sub)]


def _split2(x):
    hi = x.astype(BF16)
    lo = (x - hi.astype(F32)).astype(BF16)
    return hi, lo


def _split3(x):
    h1 = x.astype(BF16)
    r1 = x - h1.astype(F32)
    h2 = r1.astype(BF16)
    h3 = (r1 - h2.astype(F32)).astype(BF16)
    return h1, h2, h3


def _rmsnorm_kernel(x_ref, w_ref, o_ref):
    x = x_ref[...]
    y = x * lax.rsqrt(jnp.mean(x * x, axis=-1, keepdims=True) + EPS)
    o_ref[...] = (y * w_ref[...]).astype(o_ref.dtype)


def _rmsnorm(x, w, out_dtype, tm):
    m, d = x.shape
    return pl.pallas_call(
        _rmsnorm_kernel,
        grid=(m // tm,),
        in_specs=[pl.BlockSpec((tm, d), lambda i: (i, 0)),
                  pl.BlockSpec((1, d), lambda i: (0, 0))],
        out_specs=pl.BlockSpec((tm, d), lambda i: (i, 0)),
        out_shape=jax.ShapeDtypeStruct((m, d), out_dtype),
        compiler_params=_params("parallel"),
        name="rmsnorm",
    )(x, w.reshape(1, d))


def _is_last(axis):
    return pl.program_id(axis) == pl.num_programs(axis) - 1


def _proj_kernel(a_ref, as_ref, w_ref, wg_ref, wgt_ref, o_ref, ab_ref, abt_ref, k_ref, v_ref,
                 os_ref, abs_ref, abts_ref):
    o_ref[...] = _dot(a_ref[...], w_ref[...])

    @pl.when(_is_last(1))
    def _():
        os_ref[...] = _dot(as_ref[...], w_ref[...])

    @pl.when(pl.program_id(0) == 0)
    def _():
        for src, dst, dst_t in ((a_ref, ab_ref, abt_ref), (as_ref, abs_ref, abts_ref)):
            a = src[...]
            dst[...] = _dot(a, wg_ref[...])
            dst_t[...] = _dot_nt(wgt_ref[...], a)

    rows = o_ref.shape[0]
    for blk, ref in ((1, k_ref), (2, v_ref)):
        @pl.when(pl.program_id(0) == blk)
        def _(ref=ref):
            for h in range(SB_HEADS):
                ref[pl.ds(h, rows, stride=SB_HEADS), :] = o_ref[:, h * HEAD_DIM:(h + 1) * HEAD_DIM]


def _proj(a, a_s, w_in):
    m, d = a.shape
    ms = a_s.shape[0]
    tm = tn = WIDE_TILE
    assert m % tm == 0 and tn == SB_WIDTH
    n_m = m // tm

    def during(blk):
        return lambda j, i: jnp.where(j == blk, i, jnp.where(j < blk, 0, n_m - 1))

    head_major = lambda blk: pl.BlockSpec((tm * SB_HEADS, HEAD_DIM),
                                          lambda j, i: (during(blk)(j, i), 0))
    w_bf = w_in.astype(BF16)
    w_tail = w_bf[0, :, w_in.shape[2] - LANES:]
    proj, ab, abt, sb_k, sb_v, proj_s, ab_s, abt_s = pl.pallas_call(
        _proj_kernel,
        grid=(PROJ_MAIN // tn, n_m),
        in_specs=[pl.BlockSpec((tm, d), lambda j, i: (i, 0)),
                  pl.BlockSpec((ms, d), lambda j, i: (0, 0)),
                  pl.BlockSpec((None, d, tn), lambda j, i: (0, 0, j)),
                  pl.BlockSpec((d, LANES), lambda j, i: (0, 0)),
                  pl.BlockSpec((LANES, d), lambda j, i: (0, 0))],
        out_specs=[pl.BlockSpec((tm, tn), lambda j, i: (i, j)),
                   pl.BlockSpec((tm, LANES), lambda j, i: (during(0)(j, i), 0)),
                   pl.BlockSpec((LANES, tm), lambda j, i: (0, during(0)(j, i))),
                   head_major(1), head_major(2),
                   pl.BlockSpec((ms, tn), lambda j, i: (0, j)),
                   pl.BlockSpec((ms, LANES), lambda j, i: (0, 0)),
                   pl.BlockSpec((LANES, ms), lambda j, i: (0, 0))],
        out_shape=[jax.ShapeDtypeStruct((m, PROJ_MAIN), F32),
                   jax.ShapeDtypeStruct((m, LANES), F32),
                   jax.ShapeDtypeStruct((LANES, m), F32),
                   jax.ShapeDtypeStruct((m * SB_HEADS, HEAD_DIM), F32),
                   jax.ShapeDtypeStruct((m * SB_HEADS, HEAD_DIM), F32),
                   jax.ShapeDtypeStruct((ms, PROJ_MAIN), F32),
                   jax.ShapeDtypeStruct((ms, LANES), F32),
                   jax.ShapeDtypeStruct((LANES, ms), F32)],
        compiler_params=_params("arbitrary", "arbitrary"),
        name="proj",
    )(a, a_s, w_bf, w_tail, w_tail.T)
    heads = lambda x, rows: x.reshape(rows, SB_HEADS, HEAD_DIM)
    return ((proj, ab, abt, heads(sb_k, m), heads(sb_v, m)),
            (proj_s, ab_s, abt_s, heads(proj_s[:, SB_WIDTH:2 * SB_WIDTH], ms),
             heads(proj_s[:, 2 * SB_WIDTH:3 * SB_WIDTH], ms)))


def _cumsum_weights():
    j = lax.broadcasted_iota(jnp.int32, (LANES, LANES), 0)
    s = lax.broadcasted_iota(jnp.int32, (LANES, LANES), 1)
    incl = (j >= s).astype(BF16)
    half = jnp.concatenate([incl, jnp.ones((LANES, LANES), BF16)], axis=1)
    return jnp.concatenate([half, half], axis=0)


def _sb_chunk(z, vc, w2, r, diag):
    n = vc.shape[0] // LANES
    blocks = []
    for c in range(n):
        row0 = c * LANES if diag else 0
        zc = z[row0:, c * LANES:(c + 1) * LANES]
        sp = _softplus2(zc)
        valid = None
        if diag:
            valid = (lax.broadcasted_iota(jnp.int32, zc.shape, 1)
                     < lax.broadcasted_iota(jnp.int32, zc.shape, 0))
            sp = jnp.where(valid, sp, 0.0)
        hi, lo = _split2(sp)
        blocks.append((row0, zc, valid, jnp.concatenate([hi, lo], axis=1)))
    res = _dot(jnp.concatenate([blk[3] for blk in blocks], axis=0), w2)
    ends = []
    for blk in blocks:
        ends.append((ends[-1] if ends else 0) + blk[1].shape[0])
    a_parts = [None] * n
    for c in reversed(range(n)):
        row0, zc, valid, _ = blocks[c]
        rc = res[ends[c] - zc.shape[0]:ends[c]]
        a = jnp.exp2(zc - rc[:, :LANES] - r[row0:])
        if diag:
            a = jnp.where(valid, a, 0.0)
        a = a.astype(BF16)
        r_new = r[row0:] + rc[:, LANES:]
        if row0:
            a = jnp.concatenate([jnp.zeros((row0, LANES), BF16), a], axis=0)
            r_new = jnp.concatenate([r[:row0], r_new], axis=0)
        a_parts[c] = a
        r = r_new
    return _dot(jnp.concatenate(a_parts, axis=1), vc), r


def _sb_prompt_kernel(c_ref, q_ref, k_ref, v_ref, w2_ref, nw_ref, o_ref, kbf, vbf, acc, run, zbuf):
    h = pl.program_id(1)
    i = pl.program_id(2)

    @pl.when(i == 0)
    def _():
        kbf[...] = k_ref[...].astype(BF16)
        vbf[...] = v_ref[...].astype(BF16)

    z_bias = c_ref[h] * LOG2_E
    w2 = w2_ref[...]
    q = (q_ref[...] * (SB_SCALE * LOG2_E)).astype(BF16)

    def rows_of(chunk):
        return pl.ds(pl.multiple_of(jnp.maximum(chunk, 0) * SB_TQ, SB_TQ), SB_TQ)

    def logits(chunk):
        return _dot_nt(q, kbf[rows_of(chunk), :]) + z_bias

    zbuf[0] = logits(i - 1)
    d, r = _sb_chunk(logits(i), vbf[rows_of(i), :], w2, jnp.zeros(run.shape, F32), True)
    acc[...] = d
    run[...] = r

    def step(chunk, slot, more):
        if more:
            zbuf[1 - slot] = logits(chunk - 1)
        d, r = _sb_chunk(zbuf[slot], vbf[rows_of(chunk), :], w2, run[...], False)
        acc[...] += d
        run[...] = r

    def body(j, carry):
        step(i - 1 - 2 * j, 0, True)
        step(i - 2 - 2 * j, 1, True)
        return carry

    lax.fori_loop(0, i // 2, body, 0)

    @pl.when(i % 2 == 1)
    def _():
        step(0, 0, False)

    o = acc[...]
    y = o * lax.rsqrt(jnp.mean(o * o, axis=-1, keepdims=True) + EPS) * nw_ref[...]
    o_ref[...] = y.astype(o_ref.dtype)


def _sb_prompt(proj, logit_bias, out_norm, b, t):
    nq = t // SB_TQ
    blk = lambda rows, fn: pl.BlockSpec((rows, HEAD_DIM), fn)
    return pl.pallas_call(
        _sb_prompt_kernel,
        grid=(b, SB_HEADS, nq),
        in_specs=[pl.BlockSpec(memory_space=pltpu.SMEM),
                  blk(SB_TQ, lambda bi, h, i: (bi * nq + i, h)),
                  blk(t, lambda bi, h, i: (bi, SB_HEADS + h)),
                  blk(t, lambda bi, h, i: (bi, 2 * SB_HEADS + h)),
                  pl.BlockSpec((2 * LANES, 2 * LANES), lambda bi, h, i: (0, 0)),
                  pl.BlockSpec((1, HEAD_DIM), lambda bi, h, i: (0, 0))],
        out_specs=blk(SB_TQ, lambda bi, h, i: (bi * nq + i, h)),
        out_shape=jax.ShapeDtypeStruct((b * t, SB_WIDTH), BF16),
        scratch_shapes=[pltpu.VMEM((t, HEAD_DIM), BF16), pltpu.VMEM((t, HEAD_DIM), BF16),
                        pltpu.VMEM((SB_TQ, HEAD_DIM), F32), pltpu.VMEM((SB_TQ, LANES), F32),
                        pltpu.VMEM((2, SB_TQ, SB_TQ), F32)],
        compiler_params=_params("arbitrary", "arbitrary", "arbitrary"),
        name="sb_prompt",
    )(logit_bias, proj, proj, proj, _cumsum_weights(), out_norm.reshape(1, HEAD_DIM))


def _sb_decode_kernel(pt_ref, q_ref, kn_ref, vn_ref, *rest, past_len):
    k_refs = rest[:SB_PAGES_PER_STEP]
    v_refs = rest[SB_PAGES_PER_STEP:2 * SB_PAGES_PER_STEP]
    c_ref, nw_ref, o_ref, acc, run = rest[2 * SB_PAGES_PER_STEP:]
    p = pl.program_id(1)
    page = k_refs[0].shape[1]
    flat = SB_PAGES_PER_STEP * page * SB_HEADS
    lane = lax.broadcasted_iota(jnp.int32, (SB_HEADS, flat), 1)
    own = (lane % SB_HEADS) == lax.broadcasted_iota(jnp.int32, (SB_HEADS, flat), 0)
    z_bias = c_ref[...][:, :1]
    qb = (q_ref[0] * SB_SCALE).astype(BF16)

    @pl.when(p == 0)
    def _():
        kn = kn_ref[0].astype(BF16).astype(F32)
        z = jnp.sum(qb.astype(F32) * kn, axis=-1, keepdims=True) + z_bias
        valid = jnp.full(z.shape, past_len, jnp.int32) < past_len
        sp = jnp.where(valid, _softplus(z), 0.0)
        a = jnp.where(valid, jnp.exp(z - sp), 0.0)
        acc[...] = a.astype(BF16).astype(F32) * vn_ref[0].astype(BF16).astype(F32)
        run[...] = jnp.broadcast_to(sp, run.shape)

    flatten = lambda ref: ref[0].reshape(page * SB_HEADS, HEAD_DIM).astype(BF16)
    kp = jnp.concatenate([flatten(ref) for ref in reversed(k_refs)], axis=0)
    vp = jnp.concatenate([flatten(ref) for ref in reversed(v_refs)], axis=0)
    z = _dot_nt(qb, kp) + z_bias
    sp = jnp.where(own, _softplus(z), 0.0)
    s = sp
    step = SB_HEADS
    while step < flat:
        s = s + jnp.where(lane + step < flat, pltpu.roll(s, flat - step, axis=1), 0.0)
        step *= 2
    r = run[...]
    a = jnp.where(own, jnp.exp(z - s - r[:, :1]), 0.0)
    run[...] = r + jnp.sum(sp, axis=1, keepdims=True)
    acc[...] += _dot(a.astype(BF16), vp)

    @pl.when(p == pl.num_programs(1) - 1)
    def _():
        o = acc[...]
        y = o * lax.rsqrt(jnp.mean(o * o, axis=-1, keepdims=True) + EPS) * nw_ref[...]
        o_ref[0] = y.astype(o_ref.dtype)


def _sb_decode(proj, cache_k, cache_v, page_table, logit_bias, out_norm):
    b = proj.shape[0]
    _, page, _, _ = cache_k.shape
    n_pages = page_table.shape[1]
    pps = SB_PAGES_PER_STEP
    assert n_pages % pps == 0
    proj3 = proj.reshape(b, PROJ_MAIN // HEAD_DIM, HEAD_DIM)
    row = lambda blk: pl.BlockSpec((1, SB_HEADS, HEAD_DIM), lambda bi, p, pt: (bi, blk, 0))
    pg = lambda s: pl.BlockSpec(
        (1, page, SB_HEADS, HEAD_DIM),
        lambda bi, p, pt: (pt[bi * n_pages + n_pages - 1 - (p * pps + s)], 0, 0, 0))
    pages = [pg(s) for s in range(pps)]
    const = lambda shape: pl.BlockSpec(shape, lambda bi, p, pt: (0,) * len(shape))
    grid_spec = pltpu.PrefetchScalarGridSpec(
        num_scalar_prefetch=1,
        grid=(b, n_pages // pps),
        in_specs=[row(0), row(1), row(2), *pages, *pages,
                  const((SB_HEADS, LANES)), const((1, HEAD_DIM))],
        out_specs=pl.BlockSpec((1, SB_HEADS, HEAD_DIM), lambda bi, p, pt: (bi, 0, 0)),
        scratch_shapes=[pltpu.VMEM((SB_HEADS, HEAD_DIM), F32),
                        pltpu.VMEM((SB_HEADS, LANES), F32)],
    )
    out = pl.pallas_call(
        functools.partial(_sb_decode_kernel, past_len=n_pages * page),
        grid_spec=grid_spec,
        out_shape=jax.ShapeDtypeStruct((b, SB_HEADS, HEAD_DIM), BF16),
        compiler_params=_params("arbitrary", "arbitrary"),
        name="sb_decode",
    )(page_table.reshape(-1), proj3, proj3, proj3, *([cache_k] * pps), *([cache_v] * pps),
      jnp.broadcast_to(logit_bias[:, None], (SB_HEADS, LANES)), out_norm.reshape(1, HEAD_DIM))
    return out.reshape(b, SB_WIDTH)


def _shift_rows(x, prev, k):
    xr = pltpu.roll(x, k, axis=0)
    row = lax.broadcasted_iota(jnp.int32, prev.shape, 0)
    top = jnp.where(row < k, pltpu.roll(prev, k, axis=0), xr[:SUBLANES])
    return jnp.concatenate([top, xr[SUBLANES:]], axis=0)


def _gdn_kernel(cin_ref, z_ref, ab_ref, abt_ref, cst_ref, s0_ref, cw_ref, prow_ref, pcol_ref,
                nw_ref, o_ref, sfin_ref, s_scr, hist_scr, *, t_valid, t_total):
    c_len = cin_ref.shape[0]
    t = pl.program_id(1)

    @pl.when(t == 0)
    def _():
        s_scr[...] = s0_ref[0]
        hist_scr[...] = cst_ref[0]

    x = cin_ref[...]
    hist = hist_scr[...]
    cw = cw_ref[...]
    y = x * cw[3:4]
    for kk in range(1, GDN_CONV):
        y = y + _shift_rows(x, hist, kk) * cw[GDN_CONV - 1 - kk:GDN_CONV - kk]
    hist_scr[...] = x[c_len - SUBLANES:]
    conv = y * _sigmoid(y)

    ri = lax.broadcasted_iota(jnp.int32, (c_len, c_len), 0)
    ci = lax.broadcasted_iota(jnp.int32, (c_len, c_len), 1)
    tri = ri >= ci
    strict = ri > ci
    eye = (ri == ci).astype(F32)
    eye_dim = (lax.broadcasted_iota(jnp.int32, (HEAD_DIM, HEAD_DIM), 0)
               == lax.broadcasted_iota(jnp.int32, (HEAD_DIM, HEAD_DIM), 1)).astype(BF16)
    ltri = tri.astype(BF16)
    utri = (ri <= ci).astype(BF16)

    ab = ab_ref[...]
    g_col = -jnp.exp(prow_ref[0:1]) * _softplus(ab + prow_ref[1:2])
    beta_col = _sigmoid(ab)
    abt = abt_ref[0]
    g_row = (-jnp.exp(pcol_ref[0:GDN_HEADS, :c_len]) *
             _softplus(abt[0:GDN_HEADS] + pcol_ref[GDN_HEADS:2 * GDN_HEADS, :c_len]))
    if t_valid < t_total:
        pos_c = t * c_len + lax.broadcasted_iota(jnp.int32, (c_len, LANES), 0)
        g_col = jnp.where(pos_c < t_valid, g_col, 0.0)
        beta_col = jnp.where(pos_c < t_valid, beta_col, 0.0)
        pos_r = t * c_len + lax.broadcasted_iota(jnp.int32, (GDN_HEADS, c_len), 1)
        g_row = jnp.where(pos_r < t_valid, g_row, 0.0)
    gc_col = sum(_dot(ltri, part) for part in _split3(g_col))
    gc_row = sum(_dot(part, utri) for part in _split3(g_row))

    levels = []
    m = 1
    while m < c_len:
        lm = m.bit_length() - 1
        same = ((ri ^ ci) >> (lm + 1)) == 0
        lower = ((ri >> lm) & 1) > ((ci >> lm) & 1)
        levels.append(jnp.where(same, jnp.where(lower, 1.0, 0.0), 0.0))
        m *= 2

    heads = range(GDN_HEADS)
    head_cols = lambda base, h: slice(base + h * HEAD_DIM, base + (h + 1) * HEAD_DIM)
    qn, kn, kn_bf, kb, vb, decay, egc, gcc = [], [], [], [], [], [], [], []
    for h in heads:
        q = conv[:, head_cols(0, h)]
        k = conv[:, head_cols(GDN_WIDTH, h)]
        v = conv[:, head_cols(2 * GDN_WIDTH, h)]
        qn.append(q * lax.rsqrt(jnp.sum(q * q, axis=-1, keepdims=True) + 1e-6) * (HEAD_DIM ** -0.5))
        kn.append(k * lax.rsqrt(jnp.sum(k * k, axis=-1, keepdims=True) + 1e-6))
        kn_bf.append(kn[h].astype(BF16))
        gcc.append(gc_col[:, A_LANE + h:A_LANE + h + 1])
        beta = beta_col[:, B_LANE + h:B_LANE + h + 1]
        gcr = gc_row[h:h + 1, :]
        kb.append(kn[h] * beta)
        vb.append(v * beta)
        decay.append(jnp.where(tri, jnp.exp(jnp.where(tri, gcc[h] - gcr, 0.0)), 0.0))
        egc.append(jnp.exp(gcc[h]))
    low = [jnp.where(strict, _dot_nt(kb[h].astype(BF16), kn_bf[h]) * decay[h], 0.0) for h in heads]
    qk = [(_dot_nt(qn[h].astype(BF16), kn_bf[h]) * decay[h]).astype(BF16) for h in heads]

    xinv = [eye - low[h] * levels[0] for h in heads]
    for lvl in levels[1:]:
        xb = [xinv[h].astype(BF16) for h in heads]
        y1 = [_dot(xb[h], (low[h] * lvl).astype(BF16)).astype(BF16) for h in heads]
        xinv = [xinv[h] - _dot(y1[h], xb[h]) for h in heads]

    sol = []
    for h in heads:
        rhs = jnp.concatenate([vb[h], kb[h] * egc[h]], axis=1)
        sol.append(rhs + _dot((xinv[h] - eye).astype(BF16), rhs.astype(BF16)))
    s_old = [s_scr[h] for h in heads]
    ws_qs = [_dot(jnp.concatenate([sol[h][:, HEAD_DIM:], qn[h] * egc[h]], axis=0).astype(BF16),
                  s_old[h].astype(BF16)) for h in heads]
    v_new = [(sol[h][:, :HEAD_DIM] - ws_qs[h][:c_len]).astype(BF16) for h in heads]
    out = [ws_qs[h][c_len:] + _dot(qk[h], v_new[h]) for h in heads]
    nw = nw_ref[...]
    for h in heads:
        g_last = gcc[h][c_len - 1:c_len, :]
        kd = (kn[h] * jnp.exp(g_last - gcc[h])).astype(BF16)
        kd_t = _dot_nt(eye_dim, kd).astype(BF16)
        s_scr[h] = s_old[h] * jnp.exp(g_last) + _dot(kd_t, v_new[h])
    for h in heads:
        zg = z_ref[:, head_cols(0, h)]
        o = out[h]
        on = o * lax.rsqrt(jnp.mean(o * o, axis=-1, keepdims=True) + EPS) * nw
        o_ref[:, head_cols(0, h)] = (on * (zg * _sigmoid(zg))).astype(o_ref.dtype)

    @pl.when(t == pl.num_programs(1) - 1)
    def _():
        sfin_ref[0] = s_scr[...]


def _gdn(src, cin_blk, z_blk, ab, abt, conv_state, rec_state, conv_w, a_log, dt_bias, out_norm,
         b, t_pad, t_valid, c_len):
    nt = t_pad // c_len
    cst = jnp.pad(conv_state, ((0, 0), (SUBLANES - (GDN_CONV - 1), 0), (0, 0)))
    cw = jnp.pad(conv_w, ((0, SUBLANES - GDN_CONV), (0, 0)))
    prow = jnp.zeros((SUBLANES, LANES), F32)
    prow = prow.at[0, A_LANE:B_LANE].set(a_log).at[1, A_LANE:B_LANE].set(dt_bias)
    pcol = jnp.broadcast_to(jnp.concatenate([a_log, dt_bias])[:, None], (2 * GDN_HEADS, LANES))
    abt3 = abt[A_LANE:].reshape(2 * GDN_HEADS, b * nt, c_len).transpose(1, 0, 2)
    full = lambda shape: pl.BlockSpec(shape, lambda bi, ti: (0,) * len(shape))
    return pl.pallas_call(
        functools.partial(_gdn_kernel, t_valid=t_valid, t_total=t_pad),
        grid=(b, nt),
        in_specs=[pl.BlockSpec((c_len, GDN_CONV_CH), lambda bi, ti: (bi * nt + ti, cin_blk)),
                  pl.BlockSpec((c_len, GDN_WIDTH), lambda bi, ti: (bi * nt + ti, z_blk)),
                  pl.BlockSpec((c_len, LANES), lambda bi, ti: (bi * nt + ti, 0)),
                  pl.BlockSpec((1, 2 * GDN_HEADS, c_len), lambda bi, ti: (bi * nt + ti, 0, 0)),
                  pl.BlockSpec((1, SUBLANES, GDN_CONV_CH), lambda bi, ti: (bi, 0, 0)),
                  pl.BlockSpec((1, GDN_HEADS, HEAD_DIM, HEAD_DIM), lambda bi, ti: (bi, 0, 0, 0)),
                  full((SUBLANES, GDN_CONV_CH)), full((SUBLANES, LANES)),
                  full((2 * GDN_HEADS, LANES)), full((1, HEAD_DIM))],
        out_specs=[pl.BlockSpec((c_len, GDN_WIDTH), lambda bi, ti: (bi * nt + ti, 0)),
                   pl.BlockSpec((1, GDN_HEADS, HEAD_DIM, HEAD_DIM), lambda bi, ti: (bi, 0, 0, 0))],
        out_shape=[jax.ShapeDtypeStruct((b * t_pad, GDN_WIDTH), BF16),
                   jax.ShapeDtypeStruct((b, GDN_HEADS, HEAD_DIM, HEAD_DIM), F32)],
        scratch_shapes=[pltpu.VMEM((GDN_HEADS, HEAD_DIM, HEAD_DIM), F32),
                        pltpu.VMEM((SUBLANES, GDN_CONV_CH), F32)],
        compiler_params=_params("arbitrary", "arbitrary"),
        name="gdn",
    )(src, src, ab, abt3, cst, rec_state, cw, prow, pcol, out_norm.reshape(1, HEAD_DIM))


def _rms(x, w):
    return x * lax.rsqrt(jnp.mean(x * x, axis=-1, keepdims=True) + EPS) * w


def _outproj_kernel(ms_ref, mg_ref, x_ref, ms_s, mg_s, x_s, w0_ref, w1_ref, nw_ref,
                    h_ref, f_ref, h_s, f_s, w0_bf, w1_bf):
    @pl.when(pl.program_id(0) == 0)
    def _():
        w0_bf[...] = w0_ref[...].astype(BF16)
        w1_bf[...] = w1_ref[...].astype(BF16)

    def rows_out(ms, mg, x, h_out, f_out, rows):
        h = x[rows, :] + _dot(ms[rows, :], w0_bf[...]) + _dot(mg[rows, :], w1_bf[...])
        h_out[rows, :] = h
        f_out[rows, :] = _rms(h, nw_ref[...]).astype(f_out.dtype)

    for rows in _row_groups(x_ref.shape[0]):
        rows_out(ms_ref, mg_ref, x_ref, h_ref, f_ref, rows)

    @pl.when(_is_last(0))
    def _():
        rows_out(ms_s, mg_s, x_s, h_s, f_s, slice(None))


def _outproj(mix, mix_s, w_out, x, x_s, ffn_norm):
    m, d = x.shape
    ms = x_s.shape[0]
    tm = ROW_TILE
    row = lambda width: pl.BlockSpec((tm, width), lambda i: (i, 0))
    small = lambda width: pl.BlockSpec((ms, width), lambda i: (0, 0))
    h, f, h_s, f_s = pl.pallas_call(
        _outproj_kernel,
        grid=(m // tm,),
        in_specs=[row(SB_WIDTH), row(GDN_WIDTH), row(d),
                  small(SB_WIDTH), small(GDN_WIDTH), small(d),
                  pl.BlockSpec((SB_WIDTH, d), lambda i: (0, 0), pipeline_mode=pl.Buffered(1)),
                  pl.BlockSpec((GDN_WIDTH, d), lambda i: (1, 0), pipeline_mode=pl.Buffered(1)),
                  pl.BlockSpec((1, d), lambda i: (0, 0))],
        out_specs=[row(d), row(d), small(d), small(d)],
        out_shape=[jax.ShapeDtypeStruct((m, d), F32), jax.ShapeDtypeStruct((m, d), BF16),
                   jax.ShapeDtypeStruct((ms, d), F32), jax.ShapeDtypeStruct((ms, d), BF16)],
        scratch_shapes=[pltpu.VMEM((SB_WIDTH, d), BF16), pltpu.VMEM((GDN_WIDTH, d), BF16)],
        compiler_params=_params("arbitrary"),
        name="outproj",
    )(*mix, x, *mix_s, x_s, w_out, w_out, ffn_norm.reshape(1, d))
    return (h, f), (h_s, f_s)


def _ffn_up_kernel(f_ref, f_s, wg_ref, wu_ref, cw_ref, hist_ref, hist_s, act_ref, st_ref, act_s, st_s,
                   wg_bf, wu_bf, carry, *, tiles_per_seq):
    i = pl.program_id(1)

    @pl.when(i == 0)
    def _():
        wg_bf[...] = wg_ref[...].astype(BF16)
        wu_bf[...] = wu_ref[...].astype(BF16)

    cw = cw_ref[...]

    @pl.when(i % tiles_per_seq == 0)
    def _():
        carry[...] = hist_ref[0]

    prev = carry[...]
    for rows in _row_groups(f_ref.shape[0]):
        sub = rows.stop - rows.start
        f = f_ref[rows, :]
        g = _dot(f, wg_bf[...])
        u = _dot(f, wu_bf[...])
        gate = (_shift_rows(g, prev, 2) * cw[0:1] + _shift_rows(g, prev, 1) * cw[1:2]
                + g * cw[2:3])
        act_ref[rows, :] = (gate * _sigmoid(gate) * u).astype(act_ref.dtype)
        prev = g[sub - SUBLANES:]
    carry[...] = prev
    st_ref[0] = prev

    @pl.when(_is_last(1))
    def _():
        f = f_s[...]
        g = _dot(f, wg_bf[...])
        u = _dot(f, wu_bf[...])
        h0 = hist_s[0]
        h1 = hist_s[1]
        gate = h0 * cw[0:1] + h1 * cw[1:2] + g * cw[2:3]
        st_s[0] = h1
        st_s[1] = g
        act_s[...] = (gate * _sigmoid(gate) * u).astype(act_s.dtype)


def _ffn_up(f, f_s, w_gate, w_up, conv_w, state, state_s, seq_len):
    m, d = f.shape
    ms = f_s.shape[0]
    dff = w_gate.shape[1]
    tm = WIDE_TILE
    tn = COL_TILE
    keep = FFN_CONV - 1
    tiles_per_seq = seq_len // tm
    assert seq_len % tm == 0
    cw = jnp.pad(conv_w, ((0, SUBLANES - FFN_CONV), (0, 0)))
    hist = jnp.pad(state, ((0, 0), (SUBLANES - keep, 0), (0, 0)))
    seq_blk = pl.BlockSpec((1, SUBLANES, tn), lambda j, i: (i // tiles_per_seq, 0, j))
    hist_s_blk = pl.BlockSpec((keep, ms, tn), lambda j, i: (0, 0, j))
    act, st, act_s, st_s = pl.pallas_call(
        functools.partial(_ffn_up_kernel, tiles_per_seq=tiles_per_seq),
        grid=(pl.cdiv(dff, tn), m // tm),
        in_specs=[pl.BlockSpec((tm, d), lambda j, i: (i, 0)),
                  pl.BlockSpec((ms, d), lambda j, i: (0, 0)),
                  pl.BlockSpec((d, tn), lambda j, i: (0, j)),
                  pl.BlockSpec((d, tn), lambda j, i: (0, j)),
                  pl.BlockSpec((SUBLANES, tn), lambda j, i: (0, j)),
                  seq_blk, hist_s_blk],
        out_specs=[pl.BlockSpec((tm, tn), lambda j, i: (i, j)), seq_blk,
                   pl.BlockSpec((ms, tn), lambda j, i: (0, j)), hist_s_blk],
        out_shape=[jax.ShapeDtypeStruct((m, dff), BF16),
                   jax.ShapeDtypeStruct((m // seq_len, SUBLANES, dff), F32),
                   jax.ShapeDtypeStruct((ms, dff), BF16),
                   jax.ShapeDtypeStruct((keep, ms, dff), F32)],
        scratch_shapes=[pltpu.VMEM((d, tn), BF16), pltpu.VMEM((d, tn), BF16),
                        pltpu.VMEM((SUBLANES, tn), F32)],
        compiler_params=_params("arbitrary", "arbitrary"),
        name="ffn_up",
    )(f, f_s, w_gate, w_up, cw, hist, state_s.transpose(1, 0, 2))
    return (act, st[:, SUBLANES - keep:]), (act_s, st_s.transpose(1, 0, 2))


def _ffn_down_kernel(act_ref, act_s, w_ref, h_ref, h_s, o_ref, o_s, w_bf):
    @pl.when(pl.program_id(1) == 0)
    def _():
        w_bf[...] = w_ref[...].astype(BF16)

    o_ref[...] = h_ref[...] + _dot(act_ref[...], w_bf[...])

    @pl.when(_is_last(1))
    def _():
        o_s[...] = h_s[...] + _dot(act_s[...], w_bf[...])


def _ffn_down(act, act_s, w_down, h, h_s):
    m, dff = act.shape
    ms = act_s.shape[0]
    d = h.shape[1]
    tm, tn = ROW_TILE, WIDE_TILE
    return pl.pallas_call(
        _ffn_down_kernel,
        grid=(d // tn, m // tm),
        in_specs=[pl.BlockSpec((tm, dff), lambda j, i: (i, 0)),
                  pl.BlockSpec((ms, dff), lambda j, i: (0, 0)),
                  pl.BlockSpec((dff, tn), lambda j, i: (0, j), pipeline_mode=pl.Buffered(1)),
                  pl.BlockSpec((tm, tn), lambda j, i: (i, j)),
                  pl.BlockSpec((ms, tn), lambda j, i: (0, j))],
        out_specs=[pl.BlockSpec((tm, tn), lambda j, i: (i, j)),
                   pl.BlockSpec((ms, tn), lambda j, i: (0, j))],
        out_shape=[jax.ShapeDtypeStruct((m, d), F32), jax.ShapeDtypeStruct((ms, d), F32)],
        scratch_shapes=[pltpu.VMEM((dff, tn), BF16)],
        compiler_params=_params("arbitrary", "arbitrary"),
        name="ffn_down",
    )(act, act_s, w_down, h, h_s)


def _ple_kernel(p_ref, h_ref, p_s, h_s, wg_ref, wp_ref, pn_ref, nw_ref, o_ref, o_s, wg_bf, wp_bf):
    @pl.when(pl.program_id(0) == 0)
    def _():
        wg_bf[...] = wg_ref[...].astype(BF16)
        wp_bf[...] = wp_ref[...].astype(BF16)

    def rows_out(p, h_in, out, rows):
        h = h_in[rows, :]
        gate = _sigmoid(_dot(_rms(h, pn_ref[...]).astype(BF16), wg_bf[...]))
        emb = _dot(p[rows, :].astype(BF16), wp_bf[...])
        out[rows, :] = _rms(h + emb * gate, nw_ref[...])

    for rows in _row_groups(h_ref.shape[0]):
        rows_out(p_ref, h_ref, o_ref, rows)

    @pl.when(_is_last(0))
    def _():
        rows_out(p_s, h_s, o_s, slice(None))


def _ple(p, p_s, w_gate, w_proj, h, h_s, ple_norm, final_norm):
    m, d = h.shape
    ms = h_s.shape[0]
    pd = p.shape[1]
    tm = ROW_TILE
    row = lambda width: pl.BlockSpec((tm, width), lambda i: (i, 0))
    small = lambda width: pl.BlockSpec((ms, width), lambda i: (0, 0))
    return pl.pallas_call(
        _ple_kernel,
        grid=(m // tm,),
        in_specs=[row(pd), row(d), small(pd), small(d),
                  pl.BlockSpec((d, d), lambda i: (0, 0), pipeline_mode=pl.Buffered(1)),
                  pl.BlockSpec((pd, d), lambda i: (0, 0), pipeline_mode=pl.Buffered(1)),
                  pl.BlockSpec((1, d), lambda i: (0, 0)),
                  pl.BlockSpec((1, d), lambda i: (0, 0))],
        out_specs=[row(d), small(d)],
        out_shape=[jax.ShapeDtypeStruct((m, d), F32), jax.ShapeDtypeStruct((ms, d), F32)],
        scratch_shapes=[pltpu.VMEM((d, d), BF16), pltpu.VMEM((pd, d), BF16)],
        compiler_params=_params("arbitrary"),
        name="ple",
    )(p, h, p_s, h_s, w_gate, w_proj, ple_norm.reshape(1, d), final_norm.reshape(1, d))


def _gdn_group(proj, ab, abt, conv_state, rec_state, gdn_w, b, t):
    conv_w, a_log, dt_bias, out_norm = gdn_w
    if t % GDN_C == 0:
        mix, rec = _gdn(proj, O_GDN_QKV // GDN_CONV_CH, O_GDN_Z // GDN_WIDTH, ab, abt,
                        conv_state, rec_state, conv_w, a_log, dt_bias, out_norm, b, t, t, GDN_C)
    else:
        assert t <= GDN_C_SHORT
        t_pad = GDN_C_SHORT
        pad_rows = lambda arr: jnp.pad(arr.reshape(b, t, -1), ((0, 0), (0, t_pad - t), (0, 0))
                                       ).reshape(b * t_pad, -1)
        abt_pad = jnp.pad(abt.reshape(LANES, b, t), ((0, 0), (0, 0), (0, t_pad - t))
                          ).reshape(LANES, b * t_pad)
        mix, rec = _gdn(pad_rows(proj[:, O_GDN_QKV:]), 0, GDN_CONV_CH // GDN_WIDTH, pad_rows(ab),
                        abt_pad, conv_state, rec_state, conv_w, a_log, dt_bias, out_norm,
                        b, t_pad, t, GDN_C_SHORT)
        mix = mix.reshape(b, t_pad, GDN_WIDTH)[:, :t].reshape(b * t, GDN_WIDTH)
    keep = GDN_CONV - 1
    conv_in = proj.reshape(b, t, PROJ_MAIN)[:, :, O_GDN_QKV:O_GDN_Z]
    new_conv = jnp.concatenate([conv_state[:, min(t, keep):], conv_in[:, max(t - keep, 0):]],
                               axis=1)
    return mix, rec, new_conv


def kernel(x_prompt, x_sample, cache_sb_k, cache_sb_v, page_table, state_gdn_conv, state_gdn_rec, state_ffn_conv, p_prompt, p_sample, attn_norm, w_in, sb_logit_bias, sb_out_norm, gdn_conv_w, gdn_a_log, gdn_dt_bias, gdn_out_norm, w_out, ffn_norm, w_ffn_gate, w_ffn_up, ffn_conv_w, w_ffn_down, ple_norm, w_ple_gate, w_ple_proj, final_norm):
    assert attn_norm.shape[0] == 1, "single-layer step"
    assert x_sample.shape[1] == 1, "decode group carries one token per sequence"
    bp, tp, d = x_prompt.shape
    bs, ts, _ = x_sample.shape
    mp, ms = bp * tp, bs * ts
    dff = w_ffn_gate.shape[2]
    bias = sb_logit_bias[0]
    out_norm = sb_out_norm[0]
    gdn_w = (gdn_conv_w[0], gdn_a_log[0], gdn_dt_bias[0], gdn_out_norm[0])
    zeros = lambda *shape: jnp.zeros(shape, x_prompt.dtype)
    xp = x_prompt.reshape(mp, d)
    xs = x_sample.reshape(ms, d)

    a_p = _rmsnorm(xp, attn_norm[0], BF16, ROW_TILE)
    a_s = _rmsnorm(xs, attn_norm[0], BF16, ms)
    (proj_p, ab_p, abt_p, k_p, v_p), (proj_s, ab_s, abt_s, k_s, v_s) = _proj(a_p, a_s, w_in)

    sb_p = _sb_prompt(proj_p, bias, out_norm, bp, tp)
    gdn_p, rec_p, conv_p = _gdn_group(
        proj_p, ab_p, abt_p, zeros(bp, GDN_CONV - 1, GDN_CONV_CH),
        zeros(bp, GDN_HEADS, HEAD_DIM, HEAD_DIM), gdn_w, bp, tp)
    sb_s = _sb_decode(proj_s, cache_sb_k[0], cache_sb_v[0], page_table, bias, out_norm)
    gdn_s, rec_s, conv_s = _gdn_group(
        proj_s, ab_s, abt_s, state_gdn_conv[0], state_gdn_rec[0], gdn_w, bs, ts)

    (h_p, f_p), (h_s, f_s) = _outproj((sb_p, gdn_p), (sb_s, gdn_s), w_out[0], xp, xs, ffn_norm[0])
    (act_p, ffn_p), (act_s, ffn_s) = _ffn_up(
        f_p, f_s, w_ffn_gate[0], w_ffn_up[0], ffn_conv_w[0], zeros(bp, FFN_CONV - 1, dff),
        state_ffn_conv[0], tp)
    h_p, h_s = _ffn_down(act_p, act_s, w_ffn_down[0], h_p, h_s)
    y_p, y_s = _ple(p_prompt[0].reshape(mp, -1), p_sample[0].reshape(ms, -1), w_ple_gate[0],
                    w_ple_proj[0], h_p, h_s, ple_norm[0], final_norm)

    per_head = lambda x, b, t: x.reshape(1, b, t, SB_HEADS, HEAD_DIM)
    return (y_p.reshape(bp, tp, d), y_s.reshape(bs, ts, d),
            per_head(k_p, bp, tp), per_head(v_p, bp, tp), conv_p[None], rec_p[None], ffn_p[None],
            per_head(k_s, bs, ts), per_head(v_s, bs, ts), conv_s[None], rec_s[None], ffn_s[None])
```

```python
import functools

import jax
import jax.numpy as jnp
from jax import lax
from jax.experimental import pallas as pl
from jax.experimental.pallas import tpu as pltpu

F32 = jnp.float32
BF16 = jnp.bfloat16

EPS = 1e-6
HEAD_DIM = 128
SB_HEADS = 8
GDN_HEADS = 8
SB_WIDTH = SB_HEADS * HEAD_DIM
GDN_WIDTH = GDN_HEADS * HEAD_DIM
GDN_CONV = 4
GDN_CONV_CH = 3 * GDN_WIDTH
FFN_CONV = 3
SB_SCALE = HEAD_DIM ** -0.5
LOG2_E = 1.4426950408889634
O_GDN_QKV = 3 * SB_WIDTH
O_GDN_Z = O_GDN_QKV + GDN_CONV_CH
PROJ_MAIN = O_GDN_Z + GDN_WIDTH

LANES = 128
SUBLANES = 8
VMEM_LIMIT = 56 * 1024 * 1024
COL_TILE = 512
ROW_TILE = 512
WIDE_TILE = 1024
SUB_ROWS = 256
SB_TQ = 512
SB_PAGES_PER_STEP = 16
GDN_C = 128
GDN_C_SHORT = 16
A_LANE = LANES - 2 * GDN_HEADS
B_LANE = LANES - GDN_HEADS

_NT = (((1,), (1,)), ((), ()))


def _params(*sem):
    return pltpu.CompilerParams(dimension_semantics=sem, vmem_limit_bytes=VMEM_LIMIT)


def _dot(a, b):
    return jnp.dot(a, b, preferred_element_type=F32)


def _dot_nt(a, b):
    return lax.dot_general(a, b, _NT, preferred_element_type=F32)


def _softplus(x):
    return jnp.maximum(x, 0.0) + jnp.log1p(jnp.exp(-jnp.abs(x)))


def _softplus2(x):
    return jnp.maximum(x, 0.0) + jnp.log2(1.0 + jnp.exp2(-jnp.abs(x)))


def _sigmoid(x):
    return 1.0 / (1.0 + jnp.exp(-x))


def _row_groups(rows):
    sub = min(SUB_ROWS, rows)
    return [slice(r, r + sub) for r in range(0, rows, sub)]


def _split2(x):
    hi = x.astype(BF16)
    lo = (x - hi.astype(F32)).astype(BF16)
    return hi, lo


def _split3(x):
    h1 = x.astype(BF16)
    r1 = x - h1.astype(F32)
    h2 = r1.astype(BF16)
    h3 = (r1 - h2.astype(F32)).astype(BF16)
    return h1, h2, h3


def _rmsnorm_kernel(x_ref, w_ref, o_ref):
    x = x_ref[...]
    y = x * lax.rsqrt(jnp.mean(x * x, axis=-1, keepdims=True) + EPS)
    o_ref[...] = (y * w_ref[...]).astype(o_ref.dtype)


def _rmsnorm(x, w, out_dtype, tm):
    m, d = x.shape
    return pl.pallas_call(
        _rmsnorm_kernel,
        grid=(m // tm,),
        in_specs=[pl.BlockSpec((tm, d), lambda i: (i, 0)),
                  pl.BlockSpec((1, d), lambda i: (0, 0))],
        out_specs=pl.BlockSpec((tm, d), lambda i: (i, 0)),
        out_shape=jax.ShapeDtypeStruct((m, d), out_dtype),
        compiler_params=_params("parallel"),
        name="rmsnorm",
    )(x, w.reshape(1, d))


def _is_last(axis):
    return pl.program_id(axis) == pl.num_programs(axis) - 1


def _proj_kernel(a_ref, as_ref, w_ref, wg_ref, wgt_ref, o_ref, ab_ref, abt_ref, k_ref, v_ref,
                 os_ref, abs_ref, abts_ref):
    o_ref[...] = _dot(a_ref[...], w_ref[...])

    @pl.when(_is_last(1))
    def _():
        os_ref[...] = _dot(as_ref[...], w_ref[...])

    @pl.when(pl.program_id(0) == 0)
    def _():
        for src, dst, dst_t in ((a_ref, ab_ref, abt_ref), (as_ref, abs_ref, abts_ref)):
            a = src[...]
            dst[...] = _dot(a, wg_ref[...])
            dst_t[...] = _dot_nt(wgt_ref[...], a)

    rows = o_ref.shape[0]
    for blk, ref in ((1, k_ref), (2, v_ref)):
        @pl.when(pl.program_id(0) == blk)
        def _(ref=ref):
            for h in range(SB_HEADS):
                ref[pl.ds(h, rows, stride=SB_HEADS), :] = o_ref[:, h * HEAD_DIM:(h + 1) * HEAD_DIM]


def _proj(a, a_s, w_in):
    m, d = a.shape
    ms = a_s.shape[0]
    tm = tn = WIDE_TILE
    assert m % tm == 0 and tn == SB_WIDTH
    n_m = m // tm

    def during(blk):
        return lambda j, i: jnp.where(j == blk, i, jnp.where(j < blk, 0, n_m - 1))

    head_major = lambda blk: pl.BlockSpec((tm * SB_HEADS, HEAD_DIM),
                                          lambda j, i: (during(blk)(j, i), 0))
    w_bf = w_in.astype(BF16)
    w_tail = w_bf[0, :, w_in.shape[2] - LANES:]
    proj, ab, abt, sb_k, sb_v, proj_s, ab_s, abt_s = pl.pallas_call(
        _proj_kernel,
        grid=(PROJ_MAIN // tn, n_m),
        in_specs=[pl.BlockSpec((tm, d), lambda j, i: (i, 0)),
                  pl.BlockSpec((ms, d), lambda j, i: (0, 0)),
                  pl.BlockSpec((None, d, tn), lambda j, i: (0, 0, j)),
                  pl.BlockSpec((d, LANES), lambda j, i: (0, 0)),
                  pl.BlockSpec((LANES, d), lambda j, i: (0, 0))],
        out_specs=[pl.BlockSpec((tm, tn), lambda j, i: (i, j)),
                   pl.BlockSpec((tm, LANES), lambda j, i: (during(0)(j, i), 0)),
                   pl.BlockSpec((LANES, tm), lambda j, i: (0, during(0)(j, i))),
                   head_major(1), head_major(2),
                   pl.BlockSpec((ms, tn), lambda j, i: (0, j)),
                   pl.BlockSpec((ms, LANES), lambda j, i: (0, 0)),
                   pl.BlockSpec((LANES, ms), lambda j, i: (0, 0))],
        out_shape=[jax.ShapeDtypeStruct((m, PROJ_MAIN), F32),
                   jax.ShapeDtypeStruct((m, LANES), F32),
                   jax.ShapeDtypeStruct((LANES, m), F32),
                   jax.ShapeDtypeStruct((m * SB_HEADS, HEAD_DIM), F32),
                   jax.ShapeDtypeStruct((m * SB_HEADS, HEAD_DIM), F32),
                   jax.ShapeDtypeStruct((ms, PROJ_MAIN), F32),
                   jax.ShapeDtypeStruct((ms, LANES), F32),
                   jax.ShapeDtypeStruct((LANES, ms), F32)],
        compiler_params=_params("arbitrary", "arbitrary"),
        name="proj",
    )(a, a_s, w_bf, w_tail, w_tail.T)
    heads = lambda x, rows: x.reshape(rows, SB_HEADS, HEAD_DIM)
    return ((proj, ab, abt, heads(sb_k, m), heads(sb_v, m)),
            (proj_s, ab_s, abt_s, heads(proj_s[:, SB_WIDTH:2 * SB_WIDTH], ms),
             heads(proj_s[:, 2 * SB_WIDTH:3 * SB_WIDTH], ms)))


def _cumsum_weights():
    j = lax.broadcasted_iota(jnp.int32, (LANES, LANES), 0)
    s = lax.broadcasted_iota(jnp.int32, (LANES, LANES), 1)
    incl = (j >= s).astype(BF16)
    half = jnp.concatenate([incl, jnp.ones((LANES, LANES), BF16)], axis=1)
    return jnp.concatenate([half, half], axis=0)


def _sb_chunk(z, vc, w2, r, diag):
    n = vc.shape[0] // LANES
    blocks = []
    for c in range(n):
        row0 = c * LANES if diag else 0
        zc = z[row0:, c * LANES:(c + 1) * LANES]
        sp = _softplus2(zc)
        valid = None
        if diag:
            valid = (lax.broadcasted_iota(jnp.int32, zc.shape, 1)
                     < lax.broadcasted_iota(jnp.int32, zc.shape, 0))
            sp = jnp.where(valid, sp, 0.0)
        hi, lo = _split2(sp)
        blocks.append((row0, zc, valid, jnp.concatenate([hi, lo], axis=1)))
    res = _dot(jnp.concatenate([blk[3] for blk in blocks], axis=0), w2)
    ends = []
    for blk in blocks:
        ends.append((ends[-1] if ends else 0) + blk[1].shape[0])
    a_parts = [None] * n
    for c in reversed(range(n)):
        row0, zc, valid, _ = blocks[c]
        rc = res[ends[c] - zc.shape[0]:ends[c]]
        a = jnp.exp2(zc - rc[:, :LANES] - r[row0:])
        if diag:
            a = jnp.where(valid, a, 0.0)
        a = a.astype(BF16)
        r_new = r[row0:] + rc[:, LANES:]
        if row0:
            a = jnp.concatenate([jnp.zeros((row0, LANES), BF16), a], axis=0)
            r_new = jnp.concatenate([r[:row0], r_new], axis=0)
        a_parts[c] = a
        r = r_new
    return _dot(jnp.concatenate(a_parts, axis=1), vc), r


def _sb_prompt_kernel(c_ref, q_ref, k_ref, v_ref, w2_ref, nw_ref, o_ref, kbf, vbf, acc, run, zbuf):
    h = pl.program_id(1)
    i = pl.program_id(2)

    @pl.when(i == 0)
    def _():
        kbf[...] = k_ref[...].astype(BF16)
        vbf[...] = v_ref[...].astype(BF16)

    z_bias = c_ref[h] * LOG2_E
    w2 = w2_ref[...]
    q = (q_ref[...] * (SB_SCALE * LOG2_E)).astype(BF16)

    def rows_of(chunk):
        return pl.ds(pl.multiple_of(jnp.maximum(chunk, 0) * SB_TQ, SB_TQ), SB_TQ)

    def logits(chunk):
        return _dot_nt(q, kbf[rows_of(chunk), :]) + z_bias

    zbuf[0] = logits(i - 1)
    d, r = _sb_chunk(logits(i), vbf[rows_of(i), :], w2, jnp.zeros(run.shape, F32), True)
    acc[...] = d
    run[...] = r

    def step(chunk, slot, more):
        if more:
            zbuf[1 - slot] = logits(chunk - 1)
        d, r = _sb_chunk(zbuf[slot], vbf[rows_of(chunk), :], w2, run[...], False)
        acc[...] += d
        run[...] = r

    def body(j, carry):
        step(i - 1 - 2 * j, 0, True)
        step(i - 2 - 2 * j, 1, True)
        return carry

    lax.fori_loop(0, i // 2, body, 0)

    @pl.when(i % 2 == 1)
    def _():
        step(0, 0, False)

    o = acc[...]
    y = o * lax.rsqrt(jnp.mean(o * o, axis=-1, keepdims=True) + EPS) * nw_ref[...]
    o_ref[...] = y.astype(o_ref.dtype)


def _sb_prompt(proj, logit_bias, out_norm, b, t):
    nq = t // SB_TQ
    blk = lambda rows, fn: pl.BlockSpec((rows, HEAD_DIM), fn)
    return pl.pallas_call(
        _sb_prompt_kernel,
        grid=(b, SB_HEADS, nq),
        in_specs=[pl.BlockSpec(memory_space=pltpu.SMEM),
                  blk(SB_TQ, lambda bi, h, i: (bi * nq + i, h)),
                  blk(t, lambda bi, h, i: (bi, SB_HEADS + h)),
                  blk(t, lambda bi, h, i: (bi, 2 * SB_HEADS + h)),
                  pl.BlockSpec((2 * LANES, 2 * LANES), lambda bi, h, i: (0, 0)),
                  pl.BlockSpec((1, HEAD_DIM), lambda bi, h, i: (0, 0))],
        out_specs=blk(SB_TQ, lambda bi, h, i: (bi * nq + i, h)),
        out_shape=jax.ShapeDtypeStruct((b * t, SB_WIDTH), BF16),
        scratch_shapes=[pltpu.VMEM((t, HEAD_DIM), BF16), pltpu.VMEM((t, HEAD_DIM), BF16),
                        pltpu.VMEM((SB_TQ, HEAD_DIM), F32), pltpu.VMEM((SB_TQ, LANES), F32),
                        pltpu.VMEM((2, SB_TQ, SB_TQ), F32)],
        compiler_params=_params("arbitrary", "arbitrary", "arbitrary"),
        name="sb_prompt",
    )(logit_bias, proj, proj, proj, _cumsum_weights(), out_norm.reshape(1, HEAD_DIM))


def _sb_decode_kernel(pt_ref, q_ref, kn_ref, vn_ref, *rest, past_len):
    k_refs = rest[:SB_PAGES_PER_STEP]
    v_refs = rest[SB_PAGES_PER_STEP:2 * SB_PAGES_PER_STEP]
    c_ref, nw_ref, o_ref, acc, run = rest[2 * SB_PAGES_PER_STEP:]
    p = pl.program_id(1)
    page = k_refs[0].shape[1]
    flat = SB_PAGES_PER_STEP * page * SB_HEADS
    lane = lax.broadcasted_iota(jnp.int32, (SB_HEADS, flat), 1)
    own = (lane % SB_HEADS) == lax.broadcasted_iota(jnp.int32, (SB_HEADS, flat), 0)
    z_bias = c_ref[...][:, :1]
    qb = (q_ref[0] * SB_SCALE).astype(BF16)

    @pl.when(p == 0)
    def _():
        kn = kn_ref[0].astype(BF16).astype(F32)
        z = jnp.sum(qb.astype(F32) * kn, axis=-1, keepdims=True) + z_bias
        valid = jnp.full(z.shape, past_len, jnp.int32) < past_len
        sp = jnp.where(valid, _softplus(z), 0.0)
        a = jnp.where(valid, jnp.exp(z - sp), 0.0)
        acc[...] = a.astype(BF16).astype(F32) * vn_ref[0].astype(BF16).astype(F32)
        run[...] = jnp.broadcast_to(sp, run.shape)

    flatten = lambda ref: ref[0].reshape(page * SB_HEADS, HEAD_DIM).astype(BF16)
    kp = jnp.concatenate([flatten(ref) for ref in reversed(k_refs)], axis=0)
    vp = jnp.concatenate([flatten(ref) for ref in reversed(v_refs)], axis=0)
    z = _dot_nt(qb, kp) + z_bias
    sp = jnp.where(own, _softplus(z), 0.0)
    s = sp
    step = SB_HEADS
    while step < flat:
        s = s + jnp.where(lane + step < flat, pltpu.roll(s, flat - step, axis=1), 0.0)
        step *= 2
    r = run[...]
    a = jnp.where(own, jnp.exp(z - s - r[:, :1]), 0.0)
    run[...] = r + jnp.sum(sp, axis=1, keepdims=True)
    acc[...] += _dot(a.astype(BF16), vp)

    @pl.when(p == pl.num_programs(1) - 1)
    def _():
        o = acc[...]
        y = o * lax.rsqrt(jnp.mean(o * o, axis=-1, keepdims=True) + EPS) * nw_ref[...]
        o_ref[0] = y.astype(o_ref.dtype)


def _sb_decode(proj, cache_k, cache_v, page_table, logit_bias, out_norm):
    b = proj.shape[0]
    _, page, _, _ = cache_k.shape
    n_pages = page_table.shape[1]
    pps = SB_PAGES_PER_STEP
    assert n_pages % pps == 0
    proj3 = proj.reshape(b, PROJ_MAIN // HEAD_DIM, HEAD_DIM)
    row = lambda blk: pl.BlockSpec((1, SB_HEADS, HEAD_DIM), lambda bi, p, pt: (bi, blk, 0))
    pg = lambda s: pl.BlockSpec(
        (1, page, SB_HEADS, HEAD_DIM),
        lambda bi, p, pt: (pt[bi * n_pages + n_pages - 1 - (p * pps + s)], 0, 0, 0))
    pages = [pg(s) for s in range(pps)]
    const = lambda shape: pl.BlockSpec(shape, lambda bi, p, pt: (0,) * len(shape))
    grid_spec = pltpu.PrefetchScalarGridSpec(
        num_scalar_prefetch=1,
        grid=(b, n_pages // pps),
        in_specs=[row(0), row(1), row(2), *pages, *pages,
                  const((SB_HEADS, LANES)), const((1, HEAD_DIM))],
        out_specs=pl.BlockSpec((1, SB_HEADS, HEAD_DIM), lambda bi, p, pt: (bi, 0, 0)),
        scratch_shapes=[pltpu.VMEM((SB_HEADS, HEAD_DIM), F32),
                        pltpu.VMEM((SB_HEADS, LANES), F32)],
    )
    out = pl.pallas_call(
        functools.partial(_sb_decode_kernel, past_len=n_pages * page),
        grid_spec=grid_spec,
        out_shape=jax.ShapeDtypeStruct((b, SB_HEADS, HEAD_DIM), BF16),
        compiler_params=_params("arbitrary", "arbitrary"),
        name="sb_decode",
    )(page_table.reshape(-1), proj3, proj3, proj3, *([cache_k] * pps), *([cache_v] * pps),
      jnp.broadcast_to(logit_bias[:, None], (SB_HEADS, LANES)), out_norm.reshape(1, HEAD_DIM))
    return out.reshape(b, SB_WIDTH)


def _shift_rows(x, prev, k):
    xr = pltpu.roll(x, k, axis=0)
    row = lax.broadcasted_iota(jnp.int32, prev.shape, 0)
    top = jnp.where(row < k, pltpu.roll(prev, k, axis=0), xr[:SUBLANES])
    return jnp.concatenate([top, xr[SUBLANES:]], axis=0)


def _gdn_kernel(cin_ref, z_ref, ab_ref, abt_ref, cst_ref, s0_ref, cw_ref, prow_ref, pcol_ref,
                nw_ref, o_ref, sfin_ref, s_scr, hist_scr, *, t_valid, t_total):
    c_len = cin_ref.shape[0]
    t = pl.program_id(1)

    @pl.when(t == 0)
    def _():
        s_scr[...] = s0_ref[0]
        hist_scr[...] = cst_ref[0]

    x = cin_ref[...]
    hist = hist_scr[...]
    cw = cw_ref[...]
    y = x * cw[3:4]
    for kk in range(1, GDN_CONV):
        y = y + _shift_rows(x, hist, kk) * cw[GDN_CONV - 1 - kk:GDN_CONV - kk]
    hist_scr[...] = x[c_len - SUBLANES:]
    conv = y * _sigmoid(y)

    ri = lax.broadcasted_iota(jnp.int32, (c_len, c_len), 0)
    ci = lax.broadcasted_iota(jnp.int32, (c_len, c_len), 1)
    tri = ri >= ci
    strict = ri > ci
    eye = (ri == ci).astype(F32)
    eye_dim = (lax.broadcasted_iota(jnp.int32, (HEAD_DIM, HEAD_DIM), 0)
               == lax.broadcasted_iota(jnp.int32, (HEAD_DIM, HEAD_DIM), 1)).astype(BF16)
    ltri = tri.astype(BF16)
    utri = (ri <= ci).astype(BF16)

    ab = ab_ref[...]
    g_col = -jnp.exp(prow_ref[0:1]) * _softplus(ab + prow_ref[1:2])
    beta_col = _sigmoid(ab)
    abt = abt_ref[0]
    g_row = (-jnp.exp(pcol_ref[0:GDN_HEADS, :c_len]) *
             _softplus(abt[0:GDN_HEADS] + pcol_ref[GDN_HEADS:2 * GDN_HEADS, :c_len]))
    if t_valid < t_total:
        pos_c = t * c_len + lax.broadcasted_iota(jnp.int32, (c_len, LANES), 0)
        g_col = jnp.where(pos_c < t_valid, g_col, 0.0)
        beta_col = jnp.where(pos_c < t_valid, beta_col, 0.0)
        pos_r = t * c_len + lax.broadcasted_iota(jnp.int32, (GDN_HEADS, c_len), 1)
        g_row = jnp.where(pos_r < t_valid, g_row, 0.0)
    gc_col = sum(_dot(ltri, part) for part in _split3(g_col))
    gc_row = sum(_dot(part, utri) for part in _split3(g_row))

    levels = []
    m = 1
    while m < c_len:
        lm = m.bit_length() - 1
        same = ((ri ^ ci) >> (lm + 1)) == 0
        lower = ((ri >> lm) & 1) > ((ci >> lm) & 1)
        levels.append(jnp.where(same, jnp.where(lower, 1.0, 0.0), 0.0))
        m *= 2

    heads = range(GDN_HEADS)
    head_cols = lambda base, h: slice(base + h * HEAD_DIM, base + (h + 1) * HEAD_DIM)
    qn, kn, kn_bf, kb, vb, decay, egc, gcc = [], [], [], [], [], [], [], []
    for h in heads:
        q = conv[:, head_cols(0, h)]
        k = conv[:, head_cols(GDN_WIDTH, h)]
        v = conv[:, head_cols(2 * GDN_WIDTH, h)]
        qn.append(q * lax.rsqrt(jnp.sum(q * q, axis=-1, keepdims=True) + 1e-6) * (HEAD_DIM ** -0.5))
        kn.append(k * lax.rsqrt(jnp.sum(k * k, axis=-1, keepdims=True) + 1e-6))
        kn_bf.append(kn[h].astype(BF16))
        gcc.append(gc_col[:, A_LANE + h:A_LANE + h + 1])
        beta = beta_col[:, B_LANE + h:B_LANE + h + 1]
        gcr = gc_row[h:h + 1, :]
        kb.append(kn[h] * beta)
        vb.append(v * beta)
        decay.append(jnp.where(tri, jnp.exp(jnp.where(tri, gcc[h] - gcr, 0.0)), 0.0))
        egc.append(jnp.exp(gcc[h]))
    low = [jnp.where(strict, _dot_nt(kb[h].astype(BF16), kn_bf[h]) * decay[h], 0.0) for h in heads]
    qk = [(_dot_nt(qn[h].astype(BF16), kn_bf[h]) * decay[h]).astype(BF16) for h in heads]

    xinv = [eye - low[h] * levels[0] for h in heads]
    for lvl in levels[1:]:
        xb = [xinv[h].astype(BF16) for h in heads]
        y1 = [_dot(xb[h], (low[h] * lvl).astype(BF16)).astype(BF16) for h in heads]
        xinv = [xinv[h] - _dot(y1[h], xb[h]) for h in heads]

    sol = []
    for h in heads:
        rhs = jnp.concatenate([vb[h], kb[h] * egc[h]], axis=1)
        sol.append(rhs + _dot((xinv[h] - eye).astype(BF16), rhs.astype(BF16)))
    s_old = [s_scr[h] for h in heads]
    ws_qs = [_dot(jnp.concatenate([sol[h][:, HEAD_DIM:], qn[h] * egc[h]], axis=0).astype(BF16),
                  s_old[h].astype(BF16)) for h in heads]
    v_new = [(sol[h][:, :HEAD_DIM] - ws_qs[h][:c_len]).astype(BF16) for h in heads]
    out = [ws_qs[h][c_len:] + _dot(qk[h], v_new[h]) for h in heads]
    nw = nw_ref[...]
    for h in heads:
        g_last = gcc[h][c_len - 1:c_len, :]
        kd = (kn[h] * jnp.exp(g_last - gcc[h])).astype(BF16)
        kd_t = _dot_nt(eye_dim, kd).astype(BF16)
        s_scr[h] = s_old[h] * jnp.exp(g_last) + _dot(kd_t, v_new[h])
    for h in heads:
        zg = z_ref[:, head_cols(0, h)]
        o = out[h]
        on = o * lax.rsqrt(jnp.mean(o * o, axis=-1, keepdims=True) + EPS) * nw
        o_ref[:, head_cols(0, h)] = (on * (zg * _sigmoid(zg))).astype(o_ref.dtype)

    @pl.when(t == pl.num_programs(1) - 1)
    def _():
        sfin_ref[0] = s_scr[...]


def _gdn(src, cin_blk, z_blk, ab, abt, conv_state, rec_state, conv_w, a_log, dt_bias, out_norm,
         b, t_pad, t_valid, c_len):
    nt = t_pad // c_len
    cst = jnp.pad(conv_state, ((0, 0), (SUBLANES - (GDN_CONV - 1), 0), (0, 0)))
    cw = jnp.pad(conv_w, ((0, SUBLANES - GDN_CONV), (0, 0)))
    prow = jnp.zeros((SUBLANES, LANES), F32)
    prow = prow.at[0, A_LANE:B_LANE].set(a_log).at[1, A_LANE:B_LANE].set(dt_bias)
    pcol = jnp.broadcast_to(jnp.concatenate([a_log, dt_bias])[:, None], (2 * GDN_HEADS, LANES))
    abt3 = abt[A_LANE:].reshape(2 * GDN_HEADS, b * nt, c_len).transpose(1, 0, 2)
    full = lambda shape: pl.BlockSpec(shape, lambda bi, ti: (0,) * len(shape))
    return pl.pallas_call(
        functools.partial(_gdn_kernel, t_valid=t_valid, t_total=t_pad),
        grid=(b, nt),
        in_specs=[pl.BlockSpec((c_len, GDN_CONV_CH), lambda bi, ti: (bi * nt + ti, cin_blk)),
                  pl.BlockSpec((c_len, GDN_WIDTH), lambda bi, ti: (bi * nt + ti, z_blk)),
                  pl.BlockSpec((c_len, LANES), lambda bi, ti: (bi * nt + ti, 0)),
                  pl.BlockSpec((1, 2 * GDN_HEADS, c_len), lambda bi, ti: (bi * nt + ti, 0, 0)),
                  pl.BlockSpec((1, SUBLANES, GDN_CONV_CH), lambda bi, ti: (bi, 0, 0)),
                  pl.BlockSpec((1, GDN_HEADS, HEAD_DIM, HEAD_DIM), lambda bi, ti: (bi, 0, 0, 0)),
                  full((SUBLANES, GDN_CONV_CH)), full((SUBLANES, LANES)),
                  full((2 * GDN_HEADS, LANES)), full((1, HEAD_DIM))],
        out_specs=[pl.BlockSpec((c_len, GDN_WIDTH), lambda bi, ti: (bi * nt + ti, 0)),
                   pl.BlockSpec((1, GDN_HEADS, HEAD_DIM, HEAD_DIM), lambda bi, ti: (bi, 0, 0, 0))],
        out_shape=[jax.ShapeDtypeStruct((b * t_pad, GDN_WIDTH), BF16),
                   jax.ShapeDtypeStruct((b, GDN_HEADS, HEAD_DIM, HEAD_DIM), F32)],
        scratch_shapes=[pltpu.VMEM((GDN_HEADS, HEAD_DIM, HEAD_DIM), F32),
                        pltpu.VMEM((SUBLANES, GDN_CONV_CH), F32)],
        compiler_params=_params("arbitrary", "arbitrary"),
        name="gdn",
    )(src, src, ab, abt3, cst, rec_state, cw, prow, pcol, out_norm.reshape(1, HEAD_DIM))


def _rms(x, w):
    return x * lax.rsqrt(jnp.mean(x * x, axis=-1, keepdims=True) + EPS) * w


def _outproj_kernel(ms_ref, mg_ref, x_ref, ms_s, mg_s, x_s, w0_ref, w1_ref, nw_ref,
                    h_ref, f_ref, h_s, f_s, w0_bf, w1_bf):
    @pl.when(pl.program_id(0) == 0)
    def _():
        w0_bf[...] = w0_ref[...].astype(BF16)
        w1_bf[...] = w1_ref[...].astype(BF16)

    def rows_out(ms, mg, x, h_out, f_out, rows):
        h = x[rows, :] + _dot(ms[rows, :], w0_bf[...]) + _dot(mg[rows, :], w1_bf[...])
        h_out[rows, :] = h
        f_out[rows, :] = _rms(h, nw_ref[...]).astype(f_out.dtype)

    for rows in _row_groups(x_ref.shape[0]):
        rows_out(ms_ref, mg_ref, x_ref, h_ref, f_ref, rows)

    @pl.when(_is_last(0))
    def _():
        rows_out(ms_s, mg_s, x_s, h_s, f_s, slice(None))


def _outproj(mix, mix_s, w_out, x, x_s, ffn_norm):
    m, d = x.shape
    ms = x_s.shape[0]
    tm = ROW_TILE
    row = lambda width: pl.BlockSpec((tm, width), lambda i: (i, 0))
    small = lambda width: pl.BlockSpec((ms, width), lambda i: (0, 0))
    h, f, h_s, f_s = pl.pallas_call(
        _outproj_kernel,
        grid=(m // tm,),
        in_specs=[row(SB_WIDTH), row(GDN_WIDTH), row(d),
                  small(SB_WIDTH), small(GDN_WIDTH), small(d),
                  pl.BlockSpec((SB_WIDTH, d), lambda i: (0, 0), pipeline_mode=pl.Buffered(1)),
                  pl.BlockSpec((GDN_WIDTH, d), lambda i: (1, 0), pipeline_mode=pl.Buffered(1)),
                  pl.BlockSpec((1, d), lambda i: (0, 0))],
        out_specs=[row(d), row(d), small(d), small(d)],
        out_shape=[jax.ShapeDtypeStruct((m, d), F32), jax.ShapeDtypeStruct((m, d), BF16),
                   jax.ShapeDtypeStruct((ms, d), F32), jax.ShapeDtypeStruct((ms, d), BF16)],
        scratch_shapes=[pltpu.VMEM((SB_WIDTH, d), BF16), pltpu.VMEM((GDN_WIDTH, d), BF16)],
        compiler_params=_params("arbitrary"),
        name="outproj",
    )(*mix, x, *mix_s, x_s, w_out, w_out, ffn_norm.reshape(1, d))
    return (h, f), (h_s, f_s)


def _ffn_up_kernel(f_ref, f_s, wg_ref, wu_ref, cw_ref, hist_ref, hist_s, act_ref, st_ref, act_s, st_s,
                   wg_bf, wu_bf, carry, *, tiles_per_seq):
    i = pl.program_id(1)

    @pl.when(i == 0)
    def _():
        wg_bf[...] = wg_ref[...].astype(BF16)
        wu_bf[...] = wu_ref[...].astype(BF16)

    cw = cw_ref[...]

    @pl.when(i % tiles_per_seq == 0)
    def _():
        carry[...] = hist_ref[0]

    prev = carry[...]
    for rows in _row_groups(f_ref.shape[0]):
        sub = rows.stop - rows.start
        f = f_ref[rows, :]
        g = _dot(f, wg_bf[...])
        u = _dot(f, wu_bf[...])
        gate = (_shift_rows(g, prev, 2) * cw[0:1] + _shift_rows(g, prev, 1) * cw[1:2]
                + g * cw[2:3])
        act_ref[rows, :] = (gate * _sigmoid(gate) * u).astype(act_ref.dtype)
        prev = g[sub - SUBLANES:]
    carry[...] = prev
    st_ref[0] = prev

    @pl.when(_is_last(1))
    def _():
        f = f_s[...]
        g = _dot(f, wg_bf[...])
        u = _dot(f, wu_bf[...])
        h0 = hist_s[0]
        h1 = hist_s[1]
        gate = h0 * cw[0:1] + h1 * cw[1:2] + g * cw[2:3]
        st_s[0] = h1
        st_s[1] = g
        act_s[...] = (gate * _sigmoid(gate) * u).astype(act_s.dtype)


def _ffn_up(f, f_s, w_gate, w_up, conv_w, state, state_s, seq_len):
    m, d = f.shape
    ms = f_s.shape[0]
    dff = w_gate.shape[1]
    tm = WIDE_TILE
    tn = COL_TILE
    keep = FFN_CONV - 1
    tiles_per_seq = seq_len // tm
    assert seq_len % tm == 0
    cw = jnp.pad(conv_w, ((0, SUBLANES - FFN_CONV), (0, 0)))
    hist = jnp.pad(state, ((0, 0), (SUBLANES - keep, 0), (0, 0)))
    seq_blk = pl.BlockSpec((1, SUBLANES, tn), lambda j, i: (i // tiles_per_seq, 0, j))
    hist_s_blk = pl.BlockSpec((keep, ms, tn), lambda j, i: (0, 0, j))
    act, st, act_s, st_s = pl.pallas_call(
        functools.partial(_ffn_up_kernel, tiles_per_seq=tiles_per_seq),
        grid=(pl.cdiv(dff, tn), m // tm),
        in_specs=[pl.BlockSpec((tm, d), lambda j, i: (i, 0)),
                  pl.BlockSpec((ms, d), lambda j, i: (0, 0)),
                  pl.BlockSpec((d, tn), lambda j, i: (0, j)),
                  pl.BlockSpec((d, tn), lambda j, i: (0, j)),
                  pl.BlockSpec((SUBLANES, tn), lambda j, i: (0, j)),
                  seq_blk, hist_s_blk],
        out_specs=[pl.BlockSpec((tm, tn), lambda j, i: (i, j)), seq_blk,
                   pl.BlockSpec((ms, tn), lambda j, i: (0, j)), hist_s_blk],
        out_shape=[jax.ShapeDtypeStruct((m, dff), BF16),
                   jax.ShapeDtypeStruct((m // seq_len, SUBLANES, dff), F32),
                   jax.ShapeDtypeStruct((ms, dff), BF16),
                   jax.ShapeDtypeStruct((keep, ms, dff), F32)],
        scratch_shapes=[pltpu.VMEM((d, tn), BF16), pltpu.VMEM((d, tn), BF16),
                        pltpu.VMEM((SUBLANES, tn), F32)],
        compiler_params=_params("arbitrary", "arbitrary"),
        name="ffn_up",
    )(f, f_s, w_gate, w_up, cw, hist, state_s.transpose(1, 0, 2))
    return (act, st[:, SUBLANES - keep:]), (act_s, st_s.transpose(1, 0, 2))


def _ffn_down_kernel(act_ref, act_s, w_ref, h_ref, h_s, o_ref, o_s, w_bf):
    @pl.when(pl.program_id(1) == 0)
    def _():
        w_bf[...] = w_ref[...].astype(BF16)

    o_ref[...] = h_ref[...] + _dot(act_ref[...], w_bf[...])

    @pl.when(_is_last(1))
    def _():
        o_s[...] = h_s[...] + _dot(act_s[...], w_bf[...])


def _ffn_down(act, act_s, w_down, h, h_s):
    m, dff = act.shape
    ms = act_s.shape[0]
    d = h.shape[1]
    tm, tn = ROW_TILE, WIDE_TILE
    return pl.pallas_call(
        _ffn_down_kernel,
        grid=(d // tn, m // tm),
        in_specs=[pl.BlockSpec((tm, dff), lambda j, i: (i, 0)),
                  pl.BlockSpec((ms, dff), lambda j, i: (0, 0)),
                  pl.BlockSpec((dff, tn), lambda j, i: (0, j), pipeline_mode=pl.Buffered(1)),
                  pl.BlockSpec((tm, tn), lambda j, i: (i, j)),
                  pl.BlockSpec((ms, tn), lambda j, i: (0, j))],
        out_specs=[pl.BlockSpec((tm, tn), lambda j, i: (i, j)),
                   pl.BlockSpec((ms, tn), lambda j, i: (0, j))],
        out_shape=[jax.ShapeDtypeStruct((m, d), F32), jax.ShapeDtypeStruct((ms, d), F32)],
        scratch_shapes=[pltpu.VMEM((dff, tn), BF16)],
        compiler_params=_params("arbitrary", "arbitrary"),
        name="ffn_down",
    )(act, act_s, w_down, h, h_s)


def _ple_kernel(p_ref, h_ref, p_s, h_s, wg_ref, wp_ref, pn_ref, nw_ref, o_ref, o_s, wg_bf, wp_bf):
    @pl.when(pl.program_id(0) == 0)
    def _():
        wg_bf[...] = wg_ref[...].astype(BF16)
        wp_bf[...] = wp_ref[...].astype(BF16)

    def rows_out(p, h_in, out, rows):
        h = h_in[rows, :]
        gate = _sigmoid(_dot(_rms(h, pn_ref[...]).astype(BF16), wg_bf[...]))
        emb = _dot(p[rows, :].astype(BF16), wp_bf[...])
        out[rows, :] = _rms(h + emb * gate, nw_ref[...])

    for rows in _row_groups(h_ref.shape[0]):
        rows_out(p_ref, h_ref, o_ref, rows)

    @pl.when(_is_last(0))
    def _():
        rows_out(p_s, h_s, o_s, slice(None))


def _ple(p, p_s, w_gate, w_proj, h, h_s, ple_norm, final_norm):
    m, d = h.shape
    ms = h_s.shape[0]
    pd = p.shape[1]
    tm = ROW_TILE
    row = lambda width: pl.BlockSpec((tm, width), lambda i: (i, 0))
    small = lambda width: pl.BlockSpec((ms, width), lambda i: (0, 0))
    return pl.pallas_call(
        _ple_kernel,
        grid=(m // tm,),
        in_specs=[row(pd), row(d), small(pd), small(d),
                  pl.BlockSpec((d, d), lambda i: (0, 0), pipeline_mode=pl.Buffered(1)),
                  pl.BlockSpec((pd, d), lambda i: (0, 0), pipeline_mode=pl.Buffered(1)),
                  pl.BlockSpec((1, d), lambda i: (0, 0)),
                  pl.BlockSpec((1, d), lambda i: (0, 0))],
        out_specs=[row(d), small(d)],
        out_shape=[jax.ShapeDtypeStruct((m, d), F32), jax.ShapeDtypeStruct((ms, d), F32)],
        scratch_shapes=[pltpu.VMEM((d, d), BF16), pltpu.VMEM((pd, d), BF16)],
        compiler_params=_params("arbitrary"),
        name="ple",
    )(p, h, p_s, h_s, w_gate, w_proj, ple_norm.reshape(1, d), final_norm.reshape(1, d))


def _gdn_group(proj, ab, abt, conv_state, rec_state, gdn_w, b, t):
    conv_w, a_log, dt_bias, out_norm = gdn_w
    if t % GDN_C == 0:
        mix, rec = _gdn(proj, O_GDN_QKV // GDN_CONV_CH, O_GDN_Z // GDN_WIDTH, ab, abt,
                        conv_state, rec_state, conv_w, a_log, dt_bias, out_norm, b, t, t, GDN_C)
    else:
        assert t <= GDN_C_SHORT
        t_pad = GDN_C_SHORT
        pad_rows = lambda arr: jnp.pad(arr.reshape(b, t, -1), ((0, 0), (0, t_pad - t), (0, 0))
                                       ).reshape(b * t_pad, -1)
        abt_pad = jnp.pad(abt.reshape(LANES, b, t), ((0, 0), (0, 0), (0, t_pad - t))
                          ).reshape(LANES, b * t_pad)
        mix, rec = _gdn(pad_rows(proj[:, O_GDN_QKV:]), 0, GDN_CONV_CH // GDN_WIDTH, pad_rows(ab),
                        abt_pad, conv_state, rec_state, conv_w, a_log, dt_bias, out_norm,
                        b, t_pad, t, GDN_C_SHORT)
        mix = mix.reshape(b, t_pad, GDN_WIDTH)[:, :t].reshape(b * t, GDN_WIDTH)
    keep = GDN_CONV - 1
    conv_in = proj.reshape(b, t, PROJ_MAIN)[:, :, O_GDN_QKV:O_GDN_Z]
    new_conv = jnp.concatenate([conv_state[:, min(t, keep):], conv_in[:, max(t - keep, 0):]],
                               axis=1)
    return mix, rec, new_conv


def kernel(x_prompt, x_sample, cache_sb_k, cache_sb_v, page_table, state_gdn_conv, state_gdn_rec, state_ffn_conv, p_prompt, p_sample, attn_norm, w_in, sb_logit_bias, sb_out_norm, gdn_conv_w, gdn_a_log, gdn_dt_bias, gdn_out_norm, w_out, ffn_norm, w_ffn_gate, w_ffn_up, ffn_conv_w, w_ffn_down, ple_norm, w_ple_gate, w_ple_proj, final_norm):
    assert attn_norm.shape[0] == 1, "single-layer step"
    assert x_sample.shape[1] == 1, "decode group carries one token per sequence"
    bp, tp, d = x_prompt.shape
    bs, ts, _ = x_sample.shape
    mp, ms = bp * tp, bs * ts
    dff = w_ffn_gate.shape[2]
    bias = sb_logit_bias[0]
    out_norm = sb_out_norm[0]
    gdn_w = (gdn_conv_w[0], gdn_a_log[0], gdn_dt_bias[0], gdn_out_norm[0])
    zeros = lambda *shape: jnp.zeros(shape, x_prompt.dtype)
    xp = x_prompt.reshape(mp, d)
    xs = x_sample.reshape(ms, d)

    a_p = _rmsnorm(xp, attn_norm[0], BF16, ROW_TILE)
    a_s = _rmsnorm(xs, attn_norm[0], BF16, ms)
    (proj_p, ab_p, abt_p, k_p, v_p), (proj_s, ab_s, abt_s, k_s, v_s) = _proj(a_p, a_s, w_in)

    sb_p = _sb_prompt(proj_p, bias, out_norm, bp, tp)
    gdn_p, rec_p, conv_p = _gdn_group(
        proj_p, ab_p, abt_p, zeros(bp, GDN_CONV - 1, GDN_CONV_CH),
        zeros(bp, GDN_HEADS, HEAD_DIM, HEAD_DIM), gdn_w, bp, tp)
    sb_s = _sb_decode(proj_s, cache_sb_k[0], cache_sb_v[0], page_table, bias, out_norm)
    gdn_s, rec_s, conv_s = _gdn_group(
        proj_s, ab_s, abt_s, state_gdn_conv[0], state_gdn_rec[0], gdn_w, bs, ts)

    (h_p, f_p), (h_s, f_s) = _outproj((sb_p, gdn_p), (sb_s, gdn_s), w_out[0], xp, xs, ffn_norm[0])
    (act_p, ffn_p), (act_s, ffn_s) = _ffn_up(
        f_p, f_s, w_ffn_gate[0], w_ffn_up[0], ffn_conv_w[0], zeros(bp, FFN_CONV - 1, dff),
        state_ffn_conv[0], tp)
    h_p, h_s = _ffn_down(act_p, act_s, w_ffn_down[0], h_p, h_s)
    y_p, y_s = _ple(p_prompt[0].reshape(mp, -1), p_sample[0].reshape(ms, -1), w_ple_gate[0],
                    w_ple_proj[0], h_p, h_s, ple_norm[0], final_norm)

    per_head = lambda x, b, t: x.reshape(1, b, t, SB_HEADS, HEAD_DIM)
    return (y_p.reshape(bp, tp, d), y_s.reshape(bs, ts, d),
            per_head(k_p, bp, tp), per_head(v_p, bp, tp), conv_p[None], rec_p[None], ffn_p[None],
            per_head(k_s, bs, ts), per_head(v_s, bs, ts), conv_s[None], rec_s[None], ffn_s[None])
```
